```python
import jax, jax.numpy as jnp
from jax import lax
import numpy as np

D_MODEL = 1024
BATCH = 8
SEQ = 2048
DEPTH = 2

N_MIXERS = 2
N_HEADS = 4
QK_DIM = D_MODEL // 2 // N_HEADS
V_DIM = D_MODEL // N_HEADS
QK_W = N_HEADS * QK_DIM
V_W = N_HEADS * V_DIM
CHUNK = 64
D_FF = 4 * D_MODEL
GLA_GATE_RANK = 16
GLA_TAU = 16.0
EPS = 1e-6
MLSTM_IN = 2 * QK_W + 2 * V_W + 2 * N_HEADS
GLA_IN = 2 * QK_W + 2 * V_W + GLA_GATE_RANK
N_MLSTM_LAYERS = (DEPTH + 1) // 2
N_GLA_LAYERS = DEPTH // 2

kernel_name = 'hybrid_mlstm_gla_sqrelu'


def rmsnorm(x, g):
    xf = x.astype(jnp.float32)
    y = xf * lax.rsqrt(jnp.mean(xf * xf, axis=-1, keepdims=True) + EPS)
    return (y * g.astype(jnp.float32)).astype(x.dtype)


def head_rmsnorm(h, g):
    B, S, H, dv = h.shape
    y = h * lax.rsqrt(jnp.mean(h * h, axis=-1, keepdims=True) + EPS)
    return y.reshape(B, S, H * dv) * g.astype(jnp.float32)


def to_chunks(t):
    B, S, H, d = t.shape
    return t.reshape(B, S // CHUNK, CHUNK, H, d).transpose(1, 0, 3, 2, 4)


def gate_chunks(t):
    B, S, H = t.shape
    return t.reshape(B, S // CHUNK, CHUNK, H).transpose(1, 0, 3, 2)


def from_chunks(t):
    NC, B, H, L, d = t.shape
    return t.transpose(1, 0, 3, 2, 4).reshape(B, NC * L, H, d)


def mlstm_chunkwise(q, k, v, i_pre, f_pre):
    B, S, H, dk = q.shape
    dv = v.shape[-1]
    causal = jnp.tril(jnp.ones((CHUNK, CHUNK), dtype=bool))
    xs = (to_chunks(q), to_chunks(k), to_chunks(v), gate_chunks(i_pre), gate_chunks(jax.nn.log_sigmoid(f_pre)))

    def step(carry, inp):
        C, n, m = carry
        qj, ks, vs, li, lf = inp
        b = jnp.cumsum(lf, axis=-1)
        g = b[..., -1]
        dmat = jnp.where(causal, b[..., :, None] - b[..., None, :] + li[..., None, :], -jnp.inf)
        inter = b + m[..., None]
        m_row = jnp.maximum(inter, jnp.max(dmat, axis=-1))
        w_inter = jnp.exp(inter - m_row)
        s = jnp.einsum('bhjd,bhsd->bhjs', qj, ks) * jnp.exp(dmat - m_row[..., None])
        num = w_inter[..., None] * jnp.einsum('bhjd,bhde->bhje', qj, C) + jnp.einsum('bhjs,bhse->bhje', s, vs)
        den = w_inter * jnp.einsum('bhjd,bhd->bhj', qj, n) + jnp.sum(s, axis=-1)
        h = num / jnp.maximum(jnp.abs(den), jnp.exp(-m_row))[..., None]
        a = g[..., None] - b + li
        m_new = jnp.maximum(g + m, jnp.max(a, axis=-1))
        w_s = jnp.exp(a - m_new[..., None])
        decay = jnp.exp(g + m - m_new)
        C_new = decay[..., None, None] * C + jnp.einsum('bhs,bhsd,bhse->bhde', w_s, ks, vs)
        n_new = decay[..., None] * n + jnp.einsum('bhs,bhsd->bhd', w_s, ks)
        return (C_new, n_new, m_new), h

    init = (jnp.zeros((B, H, dk, dv), jnp.float32), jnp.zeros((B, H, dk), jnp.float32), jnp.zeros((B, H), jnp.float32))
    _, h = lax.scan(step, init, xs)
    return from_chunks(h)


def gla_chunkwise(q, k, v, log_a):
    B, S, H, dk = q.shape
    dv = v.shape[-1]
    causal = jnp.tril(jnp.ones((CHUNK, CHUNK), dtype=bool))[:, :, None]
    xs = (to_chunks(q), to_chunks(k), to_chunks(v), to_chunks(log_a))

    def step(S_state, inp):
        qj, ks, vs, la = inp
        b = jnp.cumsum(la, axis=-2)
        g = b[..., -1, :]
        rel = jnp.where(causal, b[..., :, None, :] - b[..., None, :, :], -jnp.inf)
        A = jnp.einsum('bhjd,bhsd,bhjsd->bhjs', qj, ks, jnp.exp(rel))
        o = jnp.einsum('bhjs,bhse->bhje', A, vs) + jnp.einsum('bhjd,bhde->bhje', qj * jnp.exp(b), S_state)
        S_new = jnp.exp(g)[..., None] * S_state + jnp.einsum('bhsd,bhse->bhde', ks * jnp.exp(g[..., None, :] - b), vs)
        return S_new, o

    _, o = lax.scan(step, jnp.zeros((B, H, dk, dv), jnp.float32), xs)
    return from_chunks(o)


def mlstm_mixer(xn, w_in, b_gate, out_norm_g, w_out):
    B, S, _ = xn.shape
    proj = xn @ w_in
    q, k, v, o_pre, gates = jnp.split(proj, [QK_W, 2 * QK_W, 2 * QK_W + V_W, 2 * QK_W + 2 * V_W], axis=-1)
    gates = gates.astype(jnp.float32) + b_gate.astype(jnp.float32)
    i_pre, f_pre = gates[..., :N_HEADS], gates[..., N_HEADS:]
    q = q.reshape(B, S, N_HEADS, QK_DIM).astype(jnp.float32)
    k = k.reshape(B, S, N_HEADS, QK_DIM).astype(jnp.float32) * (QK_DIM ** -0.5)
    v = v.reshape(B, S, N_HEADS, V_DIM).astype(jnp.float32)
    h_tilde = mlstm_chunkwise(q, k, v, i_pre, f_pre)
    h = jax.nn.sigmoid(o_pre.astype(jnp.float32)).reshape(B, S, N_HEADS, V_DIM) * h_tilde
    return head_rmsnorm(h, out_norm_g).astype(xn.dtype) @ w_out


def gla_mixer(xn, w_in, w_gate_up, b_gate, out_norm_g, w_out):
    B, S, _ = xn.shape
    proj = xn @ w_in
    q, k, v, r, z_low = jnp.split(proj, [QK_W, 2 * QK_W, 2 * QK_W + V_W, 2 * QK_W + 2 * V_W], axis=-1)
    z = (z_low @ w_gate_up).astype(jnp.float32) + b_gate.astype(jnp.float32)
    log_a = (jax.nn.log_sigmoid(z) / GLA_TAU).reshape(B, S, N_HEADS, QK_DIM)
    q = q.reshape(B, S, N_HEADS, QK_DIM).astype(jnp.float32) * (QK_DIM ** -0.5)
    k = k.reshape(B, S, N_HEADS, QK_DIM).astype(jnp.float32)
    v = v.reshape(B, S, N_HEADS, V_DIM).astype(jnp.float32)
    o = head_rmsnorm(gla_chunkwise(q, k, v, log_a), out_norm_g)
    return (jax.nn.silu(r.astype(jnp.float32)) * o).astype(xn.dtype) @ w_out


def sq_relu_mlp(xn, w1, w2):
    hid = jnp.square(jax.nn.relu(xn @ w1))
    return hid @ w2


def setup_inputs(seed: int = 0) -> dict:
    key = jax.random.key(seed)
    ks = jax.random.split(key, 20)
    f32 = jnp.float32

    def nrm(k, shape, fan_in):
        return jax.random.normal(k, shape, f32) * (fan_in ** -0.5)

    na, nb = N_MLSTM_LAYERS, N_GLA_LAYERS
    x = jax.random.normal(ks[0], (BATCH, SEQ, D_MODEL), f32)
    norm_mix_g = 1.0 + 0.02 * jax.random.normal(ks[1], (DEPTH, D_MODEL), f32)
    norm_ffn_g = 1.0 + 0.02 * jax.random.normal(ks[2], (DEPTH, D_MODEL), f32)
    final_norm_g = 1.0 + 0.02 * jax.random.normal(ks[3], (D_MODEL,), f32)
    mlstm_w_in = nrm(ks[4], (na, D_MODEL, MLSTM_IN), D_MODEL)
    i_bias = 0.1 * jax.random.normal(ks[5], (na, N_HEADS), f32)
    f_bias = 3.0 + 3.0 * jax.random.uniform(ks[6], (na, N_HEADS), f32)
    mlstm_b_gate = jnp.concatenate([i_bias, f_bias], axis=-1)
    mlstm_out_norm_g = 1.0 + 0.02 * jax.random.normal(ks[7], (na, V_W), f32)
    mlstm_w_out = nrm(ks[8], (na, V_W, D_MODEL), V_W)
    gla_w_in = nrm(ks[9], (nb, D_MODEL, GLA_IN), D_MODEL)
    gla_w_gate_up = nrm(ks[10], (nb, GLA_GATE_RANK, QK_W), GLA_GATE_RANK)
    gla_b_gate = 0.1 * jax.random.normal(ks[11], (nb, QK_W), f32)
    gla_out_norm_g = 1.0 + 0.02 * jax.random.normal(ks[12], (nb, V_W), f32)
    gla_w_out = nrm(ks[13], (nb, V_W, D_MODEL), V_W)
    ffn_w1 = nrm(ks[14], (DEPTH, D_MODEL, D_FF), D_MODEL)
    ffn_w2 = nrm(ks[15], (DEPTH, D_FF, D_MODEL), D_FF)
    return {'x': x, 'norm_mix_g': norm_mix_g, 'norm_ffn_g': norm_ffn_g, 'final_norm_g': final_norm_g,
            'mlstm_w_in': mlstm_w_in, 'mlstm_b_gate': mlstm_b_gate, 'mlstm_out_norm_g': mlstm_out_norm_g,
            'mlstm_w_out': mlstm_w_out, 'gla_w_in': gla_w_in, 'gla_w_gate_up': gla_w_gate_up,
            'gla_b_gate': gla_b_gate, 'gla_out_norm_g': gla_out_norm_g, 'gla_w_out': gla_w_out,
            'ffn_w1': ffn_w1, 'ffn_w2': ffn_w2}


def reference(x, norm_mix_g, norm_ffn_g, final_norm_g, mlstm_w_in, mlstm_b_gate, mlstm_out_norm_g,
              mlstm_w_out, gla_w_in, gla_w_gate_up, gla_b_gate, gla_out_norm_g, gla_w_out, ffn_w1, ffn_w2):
    for i in range(DEPTH):
        j = i // N_MIXERS
        h = rmsnorm(x, norm_mix_g[i])
        if i % N_MIXERS == 0:
            x = x + mlstm_mixer(h, mlstm_w_in[j], mlstm_b_gate[j], mlstm_out_norm_g[j], mlstm_w_out[j])
        else:
            x = x + gla_mixer(h, gla_w_in[j], gla_w_gate_up[j], gla_b_gate[j], gla_out_norm_g[j], gla_w_out[j])
        h = rmsnorm(x, norm_ffn_g[i])
        x = x + sq_relu_mlp(h, ffn_w1[i], ffn_w2[i])
    return rmsnorm(x, final_norm_g)
```

```python
import functools

import jax
import jax.numpy as jnp
from jax import lax
from jax.experimental import pallas as pl
from jax.experimental.pallas import tpu as pltpu

EPS = 1e-6
N_HEADS = 4
GLA_TAU = 16.0

LANE = 128
SUBLANE = 8
VMEM_LIMIT_BYTES = 56 * 1024 * 1024

F32 = jnp.float32
BF16 = jnp.bfloat16

_NT = (((1,), (1,)), ((), ()))
_TN = (((0,), (0,)), ((), ()))


def _dot(a, b):
    return jnp.dot(a, b, preferred_element_type=F32)


def _dot_nt(a, b):
    return lax.dot_general(a, b, _NT, preferred_element_type=F32)


def _dot_tn(a, b):
    return lax.dot_general(a, b, _TN, preferred_element_type=F32)


def _rmsnorm(x, g):
    return x * lax.rsqrt(jnp.mean(x * x, axis=-1, keepdims=True) + EPS) * g


def _log_sigmoid(x):
    return jnp.minimum(x, 0.0) - jnp.log1p(jnp.exp(-jnp.abs(x)))


def _cumsum_rows(tril, x):
    hi = x.astype(BF16)
    r1 = x - hi.astype(F32)
    mid = r1.astype(BF16)
    lo = (r1 - mid.astype(F32)).astype(BF16)
    return _dot(tril, hi) + (_dot(tril, mid) + _dot(tril, lo))


def _proj_kernel(x_ref, g_ref, w_ref, wg_ref, bg_ref, main_ref, gate_ref, *, col_scales, tn):
    xb = _rmsnorm(x_ref[...], g_ref[...]).astype(BF16)
    for c, scale in enumerate(col_scales):
        acc = _dot(xb, w_ref[:, c * tn:(c + 1) * tn])
        if scale != 1.0:
            acc = acc * scale
        main_ref[:, c * tn:(c + 1) * tn] = acc.astype(BF16)
    gate_ref[...] = _dot(xb, wg_ref[...]) + bg_ref[...]


def _input_projection(x2d, norm_g, w_main, w_gate, b_gate, col_scales, tn, tm):
    t, d = x2d.shape
    n_main = w_main.shape[1]
    const = lambda i: (0, 0)
    return pl.pallas_call(
        functools.partial(_proj_kernel, col_scales=col_scales, tn=tn),
        grid=(t // tm,),
        in_specs=[
            pl.BlockSpec((tm, d), lambda i: (i, 0)),
            pl.BlockSpec((1, d), const),
            pl.BlockSpec((d, n_main), const),
            pl.BlockSpec((d, LANE), const),
            pl.BlockSpec((1, LANE), const),
        ],
        out_specs=[
            pl.BlockSpec((tm, n_main), lambda i: (i, 0)),
            pl.BlockSpec((tm, LANE), lambda i: (i, 0)),
        ],
        out_shape=[
            jax.ShapeDtypeStruct((t, n_main), BF16),
            jax.ShapeDtypeStruct((t, LANE), F32),
        ],
        compiler_params=pltpu.CompilerParams(
            dimension_semantics=("parallel",), vmem_limit_bytes=VMEM_LIMIT_BYTES),
        name="input_projection",
    )(x2d, norm_g, w_main, w_gate, b_gate)


def _mlstm_kernel(q_ref, k_ref, v_ref, o_ref, gt_ref, gain_ref, out_ref, c_ref, n_ref, m_ref,
                  *, heads, dk, dv, chunk):
    L = chunk

    @pl.when(pl.program_id(1) == 0)
    def _():
        c_ref[...] = jnp.zeros_like(c_ref)
        n_ref[...] = jnp.zeros_like(n_ref)
        m_ref[...] = jnp.zeros_like(m_ref)

    row = lax.broadcasted_iota(jnp.int32, (L, L), 0)
    col = lax.broadcasted_iota(jnp.int32, (L, L), 1)
    causal = col <= row
    tril = jnp.where(causal, 1.0, 0.0).astype(BF16)

    gates = gt_ref[...]
    bsum = _cumsum_rows(tril, _log_sigmoid(gates))
    lane = lax.broadcasted_iota(jnp.int32, (L, LANE), 1)
    rows_t = jnp.where(lane < heads, gates, bsum).T

    for h in range(heads):
        li_col = gates[:, h:h + 1]
        b_col = bsum[:, heads + h:heads + h + 1]
        li_row = rows_t[h:h + 1, :]
        b_row = rows_t[heads + h:heads + h + 1, :]
        m_prev = m_ref[h][:, 0:1]
        g_tot = b_col[L - 1:L, :]

        dmat = jnp.where(causal, b_col + (li_row - b_row), -jnp.inf)
        inter = b_col + m_prev
        m_row = jnp.maximum(inter, jnp.max(dmat, axis=-1, keepdims=True))
        w_inter = jnp.exp(inter - m_row)
        p = jnp.exp(dmat - m_row)

        q = q_ref[:, h * dk:(h + 1) * dk]
        k = k_ref[:, h * dk:(h + 1) * dk]
        v = v_ref[:, h * dv:(h + 1) * dv]
        s = _dot_nt(q, k) * p
        c_prev = c_ref[h]
        num = w_inter * _dot(q, c_prev.astype(BF16)) + _dot(s.astype(BF16), v)
        den = (w_inter * jnp.sum(q.astype(F32) * n_ref[h], axis=-1, keepdims=True)
               + jnp.sum(s, axis=-1, keepdims=True))
        h_tilde = num * (1.0 / jnp.maximum(jnp.abs(den), jnp.exp(-m_row)))

        a = g_tot - b_col + li_col
        m_new = jnp.maximum(g_tot + m_prev, jnp.max(a, axis=0, keepdims=True))
        decay = jnp.exp(g_tot + m_prev - m_new)
        kw = k.astype(F32) * jnp.exp(a - m_new)
        c_ref[h] = decay * c_prev + _dot_tn(kw.astype(BF16), v)
        n_ref[h] = decay * n_ref[h] + jnp.sum(kw, axis=0, keepdims=True)
        m_ref[h] = jnp.broadcast_to(m_new, (1, LANE))

        gated = jax.nn.sigmoid(o_ref[:, h * dv:(h + 1) * dv].astype(F32)) * h_tilde
        out_ref[:, h * dv:(h + 1) * dv] = _rmsnorm(gated, gain_ref[:, h * dv:(h + 1) * dv]).astype(BF16)


def _mlstm_mixer(main, gates, out_gain, *, heads, qk_w, v_w, chunk):
    b, s, _ = main.shape
    assert 2 * qk_w == v_w and s % chunk == 0
    dk, dv = qk_w // heads, v_w // heads
    return pl.pallas_call(
        functools.partial(_mlstm_kernel, heads=heads, dk=dk, dv=dv, chunk=chunk),
        grid=(b, s // chunk),
        in_specs=[
            pl.BlockSpec((None, chunk, qk_w), lambda i, j: (i, j, 0)),
            pl.BlockSpec((None, chunk, qk_w), lambda i, j: (i, j, 1)),
            pl.BlockSpec((None, chunk, v_w), lambda i, j: (i, j, 1)),
            pl.BlockSpec((None, chunk, v_w), lambda i, j: (i, j, 2)),
            pl.BlockSpec((None, chunk, LANE), lambda i, j: (i, j, 0)),
            pl.BlockSpec((1, v_w), lambda i, j: (0, 0)),
        ],
        out_specs=pl.BlockSpec((None, chunk, v_w), lambda i, j: (i, j, 0)),
        out_shape=jax.ShapeDtypeStruct((b, s, v_w), BF16),
        scratch_shapes=[
            pltpu.VMEM((heads, dk, dv), F32),
            pltpu.VMEM((heads, 1, dk), F32),
            pltpu.VMEM((heads, 1, LANE), F32),
        ],
        compiler_params=pltpu.CompilerParams(
            dimension_semantics=("parallel", "arbitrary"), vmem_limit_bytes=VMEM_LIMIT_BYTES),
        name="mlstm_mixer",
    )(main, main, main, main, gates, out_gain)


def _gla_levels(chunk):
    levels, w = [], SUBLANE
    while w < chunk:
        levels.append(w)
        w *= 2
    return tuple(levels)


def _gla_kernel(q_ref, k_ref, v_ref, r_ref, zl_ref, wup_ref, bup_ref, gain_ref, out_ref, s_ref,
                *, heads, dk, dv, chunk):
    L = chunk
    levels = _gla_levels(L)

    @pl.when(pl.program_id(1) == 0)
    def _():
        s_ref[...] = jnp.zeros_like(s_ref)

    row = lax.broadcasted_iota(jnp.int32, (L, L), 0)
    col = lax.broadcasted_iota(jnp.int32, (L, L), 1)
    tril = jnp.where(col <= row, 1.0, 0.0).astype(BF16)
    diff = row ^ col
    level_idx = sum(jnp.where(diff >= w, 1, 0) for w in levels)
    code = jnp.where(col > row, -1, jnp.where(diff < SUBLANE, row - col, SUBLANE - 1 + level_idx))

    z = _dot(zl_ref[...].astype(BF16), wup_ref[...]) + bup_ref[...]
    b_all = _cumsum_rows(tril, _log_sigmoid(z) * (1.0 / GLA_TAU))

    for h in range(heads):
        b = b_all[:, h * dk:(h + 1) * dk]
        g_tot = b[L - 1:L, :]
        qf = q_ref[:, h * dk:(h + 1) * dk].astype(F32)
        kf = k_ref[:, h * dk:(h + 1) * dk].astype(F32)
        v = v_ref[:, h * dv:(h + 1) * dv]
        s_prev = s_ref[h]

        o = _dot((qf * jnp.exp(b)).astype(BF16), s_prev.astype(BF16))

        a_mat = jnp.zeros((L, L), F32)
        b3 = b.reshape(L // SUBLANE, SUBLANE, dk)
        q3 = qf.reshape(L // SUBLANE, SUBLANE, dk)
        k3 = kf.reshape(L // SUBLANE, SUBLANE, dk)
        for t in range(SUBLANE):
            if t == 0:
                x = q3 * k3
            else:
                k_sh = pltpu.roll(k3, t, 1)
                b_sh = pltpu.roll(b3, t, 1)
                x = q3 * k_sh * jnp.exp(jnp.minimum(b3 - b_sh, 0.0))
            r = jnp.sum(x, axis=-1, keepdims=True).reshape(L, 1)
            a_mat = jnp.where(code == t, r, a_mat)

        for i, w in enumerate(levels):
            nb = L // (2 * w)
            mid = b.reshape(nb, 2 * w, dk)[:, w - 1:w, :]
            mid = jnp.broadcast_to(mid, (nb, 2 * w, dk)).reshape(L, dk)
            q_t = (qf * jnp.exp(jnp.minimum(b - mid, 0.0))).astype(BF16)
            k_t = (kf * jnp.exp(jnp.minimum(mid - b, 0.0))).astype(BF16)
            a_mat = jnp.where(code == SUBLANE + i, _dot_nt(q_t, k_t), a_mat)

        o = o + _dot(a_mat.astype(BF16), v)

        kg = (kf * jnp.exp(g_tot - b)).astype(BF16)
        g_col = jnp.broadcast_to(jnp.exp(g_tot), (LANE, dk)).T
        g_col = jnp.concatenate([g_col] * (dv // LANE), axis=1)
        s_ref[h] = g_col * s_prev + _dot_tn(kg, v)

        y = _rmsnorm(o, gain_ref[:, h * dv:(h + 1) * dv])
        rr = r_ref[:, h * dv:(h + 1) * dv].astype(F32)
        out_ref[:, h * dv:(h + 1) * dv] = (rr * jax.nn.sigmoid(rr) * y).astype(BF16)


def _gla_mixer(main, z_low, w_up, b_up, out_gain, *, heads, qk_w, v_w, chunk):
    b, s, _ = main.shape
    assert 2 * qk_w == v_w and s % chunk == 0
    dk, dv = qk_w // heads, v_w // heads
    const = lambda i, j: (0, 0)
    return pl.pallas_call(
        functools.partial(_gla_kernel, heads=heads, dk=dk, dv=dv, chunk=chunk),
        grid=(b, s // chunk),
        in_specs=[
            pl.BlockSpec((None, chunk, qk_w), lambda i, j: (i, j, 0)),
            pl.BlockSpec((None, chunk, qk_w), lambda i, j: (i, j, 1)),
            pl.BlockSpec((None, chunk, v_w), lambda i, j: (i, j, 1)),
            pl.BlockSpec((None, chunk, v_w), lambda i, j: (i, j, 2)),
            pl.BlockSpec((None, chunk, LANE), lambda i, j: (i, j, 0)),
            pl.BlockSpec((LANE, qk_w), const),
            pl.BlockSpec((1, qk_w), const),
            pl.BlockSpec((1, v_w), const),
        ],
        out_specs=pl.BlockSpec((None, chunk, v_w), lambda i, j: (i, j, 0)),
        out_shape=jax.ShapeDtypeStruct((b, s, v_w), BF16),
        scratch_shapes=[pltpu.VMEM((heads, dk, dv), F32)],
        compiler_params=pltpu.CompilerParams(
            dimension_semantics=("parallel", "arbitrary"), vmem_limit_bytes=VMEM_LIMIT_BYTES),
        name="gla_mixer",
    )(main, main, main, main, z_low, w_up, b_up, out_gain)


def _out_ffn_kernel(h_ref, x_ref, wo_ref, g_ref, w1_ref, w2_ref, fg_ref, o_ref, *, ff_chunk, final_norm):
    x1 = x_ref[...] + _dot(h_ref[...], wo_ref[...])
    hn = _rmsnorm(x1, g_ref[...]).astype(BF16)
    acc = x1
    for c in range(w1_ref.shape[1] // ff_chunk):
        u = jnp.maximum(_dot(hn, w1_ref[:, c * ff_chunk:(c + 1) * ff_chunk]), 0.0)
        acc = acc + _dot((u * u).astype(BF16), w2_ref[c * ff_chunk:(c + 1) * ff_chunk, :])
    if final_norm:
        acc = _rmsnorm(acc, fg_ref[...])
    o_ref[...] = acc


def _out_ffn(h2d, x2d, w_out, norm_g, w1, w2, final_g, *, final_norm, tm, ff_chunk):
    t, d = x2d.shape
    v_w, d_ff = h2d.shape[1], w1.shape[1]
    const = lambda i: (0, 0)
    resident = pl.Buffered(1)
    return pl.pallas_call(
        functools.partial(_out_ffn_kernel, ff_chunk=ff_chunk, final_norm=final_norm),
        grid=(t // tm,),
        in_specs=[
            pl.BlockSpec((tm, v_w), lambda i: (i, 0)),
            pl.BlockSpec((tm, d), lambda i: (i, 0)),
            pl.BlockSpec((v_w, d), const, pipeline_mode=resident),
            pl.BlockSpec((1, d), const),
            pl.BlockSpec((d, d_ff), const, pipeline_mode=resident),
            pl.BlockSpec((d_ff, d), const, pipeline_mode=resident),
            pl.BlockSpec((1, d), const),
        ],
        out_specs=pl.BlockSpec((tm, d), lambda i: (i, 0)),
        out_shape=jax.ShapeDtypeStruct((t, d), F32),
        compiler_params=pltpu.CompilerParams(
            dimension_semantics=("parallel",), vmem_limit_bytes=VMEM_LIMIT_BYTES),
        name="out_proj_ffn",
    )(h2d, x2d, w_out, norm_g, w1, w2, final_g)


PROJ_ROWS = 512
FFN_ROWS = 512
FFN_CHUNK = 1024
MLSTM_CHUNK = 256
GLA_CHUNK = 128


def _pad_lanes(a):
    return jnp.pad(a, ((0, 0), (0, LANE - a.shape[1])))


def kernel(x, norm_mix_g, norm_ffn_g, final_norm_g, mlstm_w_in, mlstm_b_gate, mlstm_out_norm_g, mlstm_w_out,
           gla_w_in, gla_w_gate_up, gla_b_gate, gla_out_norm_g, gla_w_out, ffn_w1, ffn_w2):
    b, s, d = x.shape
    depth = norm_mix_g.shape[0]
    heads = N_HEADS
    qk_w, v_w = d // 2, d
    n_main = 2 * qk_w + 2 * v_w
    dk = qk_w // heads
    t = b * s
    x2d = x.reshape(t, d)
    final_g = final_norm_g.reshape(1, d)
    n_cols = n_main // qk_w

    for i in range(depth):
        j = i // 2
        norm_g = norm_mix_g[i].reshape(1, d)
        if i % 2 == 0:
            w_in = mlstm_w_in[j]
            scales = (1.0, dk ** -0.5) + (1.0,) * (n_cols - 2)
            main, gates = _input_projection(
                x2d, norm_g, w_in[:, :n_main].astype(BF16), _pad_lanes(w_in[:, n_main:]).astype(BF16),
                _pad_lanes(mlstm_b_gate[j].reshape(1, -1)), scales, qk_w, PROJ_ROWS)
            mixed = _mlstm_mixer(main.reshape(b, s, n_main), gates.reshape(b, s, LANE),
                                 mlstm_out_norm_g[j].reshape(1, v_w),
                                 heads=heads, qk_w=qk_w, v_w=v_w, chunk=MLSTM_CHUNK)
            w_out = mlstm_w_out[j]
        else:
            w_in = gla_w_in[j]
            scales = (dk ** -0.5,) + (1.0,) * (n_cols - 1)
            main, z_low = _input_projection(
                x2d, norm_g, w_in[:, :n_main].astype(BF16), _pad_lanes(w_in[:, n_main:]).astype(BF16),
                jnp.zeros((1, LANE), F32), scales, qk_w, PROJ_ROWS)
            rank = gla_w_gate_up.shape[1]
            w_up = jnp.pad(gla_w_gate_up[j], ((0, LANE - rank), (0, 0))).astype(BF16)
            mixed = _gla_mixer(main.reshape(b, s, n_main), z_low.reshape(b, s, LANE), w_up,
                               gla_b_gate[j].reshape(1, qk_w), gla_out_norm_g[j].reshape(1, v_w),
                               heads=heads, qk_w=qk_w, v_w=v_w, chunk=GLA_CHUNK)
            w_out = gla_w_out[j]
        x2d = _out_ffn(mixed.reshape(t, v_w), x2d, w_out.astype(BF16), norm_ffn_g[i].reshape(1, d),
                       ffn_w1[i].astype(BF16), ffn_w2[i].astype(BF16), final_g,
                       final_norm=(i == depth - 1), tm=FFN_ROWS, ff_chunk=FFN_CHUNK)
    return x2d.reshape(b, s, d)
```

```python
import functools

import math

import numpy as np

import jax
import jax.numpy as jnp
from jax import lax
from jax.experimental import pallas as pl
from jax.experimental.pallas import tpu as pltpu

EPS = 1e-6
N_HEADS = 4
GLA_TAU = 16.0
LOG2_E = math.log2(math.e)

LANE = 128
SUBLANE = 8
VMEM_LIMIT_BYTES = 56 * 1024 * 1024

F32 = jnp.float32
BF16 = jnp.bfloat16

_NT = (((1,), (1,)), ((), ()))
_TN = (((0,), (0,)), ((), ()))


def _dot(a, b):
    return jnp.dot(a, b, preferred_element_type=F32)


def _dot_nt(a, b):
    return lax.dot_general(a, b, _NT, preferred_element_type=F32)


def _dot_tn(a, b):
    return lax.dot_general(a, b, _TN, preferred_element_type=F32)


def _rmsnorm(x, g):
    return x * lax.rsqrt(jnp.mean(x * x, axis=-1, keepdims=True) + EPS) * g


def _log_sigmoid(x):
    return jnp.minimum(x, 0.0) - jnp.log1p(jnp.exp(-jnp.abs(x)))


def _cumsum_matrix(chunk):
    tril = np.tril(np.ones((chunk, chunk), np.float32))
    return jnp.asarray(np.concatenate([tril] * 3, axis=1), dtype=BF16)


def _cumsum_rows(tril3, x):
    hi = x.astype(BF16)
    r1 = x - hi.astype(F32)
    mid = r1.astype(BF16)
    lo = (r1 - mid.astype(F32)).astype(BF16)
    return _dot(tril3, jnp.concatenate([hi, mid, lo], axis=0))


def _proj_kernel(x_ref, g_ref, w_ref, wg_ref, bg_ref, main_ref, gate_ref, *, col_scales, tn):
    xb = _rmsnorm(x_ref[...], g_ref[...]).astype(BF16)
    for c, scale in enumerate(col_scales):
        acc = _dot(xb, w_ref[:, c * tn:(c + 1) * tn])
        if scale != 1.0:
            acc = acc * scale
        main_ref[:, c * tn:(c + 1) * tn] = acc.astype(BF16)
    gate_ref[...] = _dot(xb, wg_ref[...]) + bg_ref[...]


def _input_projection(x2d, norm_g, w_main, w_gate, b_gate, col_scales, tn, tm):
    t, d = x2d.shape
    n_main = w_main.shape[1]
    const = lambda i: (0, 0)
    return pl.pallas_call(
        functools.partial(_proj_kernel, col_scales=col_scales, tn=tn),
        grid=(t // tm,),
        in_specs=[
            pl.BlockSpec((tm, d), lambda i: (i, 0)),
            pl.BlockSpec((1, d), const),
            pl.BlockSpec((d, n_main), const),
            pl.BlockSpec((d, LANE), const),
            pl.BlockSpec((1, LANE), const),
        ],
        out_specs=[
            pl.BlockSpec((tm, n_main), lambda i: (i, 0)),
            pl.BlockSpec((tm, LANE), lambda i: (i, 0)),
        ],
        out_shape=[
            jax.ShapeDtypeStruct((t, n_main), BF16),
            jax.ShapeDtypeStruct((t, LANE), F32),
        ],
        compiler_params=pltpu.CompilerParams(
            dimension_semantics=("parallel",), vmem_limit_bytes=VMEM_LIMIT_BYTES),
        name="input_projection",
    )(x2d, norm_g, w_main, w_gate, b_gate)


def _mlstm_kernel(q_ref, k_ref, v_ref, o_ref, gt_ref, gain_ref, tril_ref, out_ref, c_ref, n_ref, m_ref,
                  *, heads, dk, dv, chunk):
    L = chunk

    @pl.when(pl.program_id(1) == 0)
    def _():
        c_ref[...] = jnp.zeros_like(c_ref)
        n_ref[...] = jnp.zeros_like(n_ref)
        m_ref[...] = jnp.zeros_like(m_ref)

    causal = lax.broadcasted_iota(jnp.int32, (L, L), 1) <= lax.broadcasted_iota(jnp.int32, (L, L), 0)
    lane = lax.broadcasted_iota(jnp.int32, (L, LANE), 1)
    head_lanes = lane < heads
    head_lane = [lane == h for h in range(heads)]

    for bi in range(q_ref.shape[0]):
        gates = gt_ref[bi]
        pre = jnp.where(head_lanes, gates, _log_sigmoid(gates)) * LOG2_E
        cum = _cumsum_rows(tril_ref[...], pre)
        rows_t = jnp.where(head_lanes, pre, cum).T
        li = jnp.where(head_lanes, pre, 0.0)
        b = jnp.where(head_lanes, pltpu.roll(cum, LANE - heads, 1), 0.0)
        m_prev = m_ref[bi]
        g_tot = b[L - 1:L, :]
        inter = b + m_prev

        dmats, row_max = [], jnp.zeros((L, LANE), F32)
        for h in range(heads):
            li_row = rows_t[h:h + 1, :]
            b_row = rows_t[heads + h:heads + h + 1, :]
            dmat = jnp.where(causal, b[:, h:h + 1] + (li_row - b_row), -jnp.inf)
            dmats.append(dmat)
            row_max = jnp.where(head_lane[h], jnp.max(dmat, axis=-1, keepdims=True), row_max)
        m_row = jnp.maximum(inter, row_max)
        w_inter = jnp.exp2(inter - m_row)

        parts, q_n, s_sum = [], jnp.zeros((L, LANE), F32), jnp.zeros((L, LANE), F32)
        for h in range(heads):
            q = q_ref[bi, :, h * dk:(h + 1) * dk]
            k = k_ref[bi, :, h * dk:(h + 1) * dk]
            v = v_ref[bi, :, h * dv:(h + 1) * dv]
            s = _dot_nt(q, k) * jnp.exp2(dmats[h] - m_row[:, h:h + 1])
            s_sum = jnp.where(head_lane[h], jnp.sum(s, axis=-1, keepdims=True), s_sum)
            q_n = jnp.where(head_lane[h], jnp.sum(q.astype(F32) * n_ref[bi, h], axis=-1, keepdims=True), q_n)
            parts.append((_dot(q, c_ref[bi, h].astype(BF16)), _dot(s.astype(BF16), v)))
        den = w_inter * q_n + s_sum
        inv = 1.0 / jnp.maximum(jnp.abs(den), jnp.exp2(-m_row))
        w_inter_inv = w_inter * inv

        a = g_tot - b + li
        m_new = jnp.maximum(g_tot + m_prev, jnp.max(a, axis=0, keepdims=True))
        w_s = jnp.exp2(a - m_new)
        decay = jnp.exp2(g_tot + m_prev - m_new)
        m_ref[bi] = m_new

        for h in range(heads):
            k = k_ref[bi, :, h * dk:(h + 1) * dk]
            v = v_ref[bi, :, h * dv:(h + 1) * dv]
            inter_part, intra_part = parts[h]
            h_tilde = inter_part * w_inter_inv[:, h:h + 1] + intra_part * inv[:, h:h + 1]
            kw = k.astype(F32) * w_s[:, h:h + 1]
            decay_h = decay[:, h:h + 1]
            c_ref[bi, h] = decay_h * c_ref[bi, h] + _dot_tn(kw.astype(BF16), v)
            n_ref[bi, h] = decay_h * n_ref[bi, h] + jnp.sum(kw, axis=0, keepdims=True)

            gated = jax.nn.sigmoid(o_ref[bi, :, h * dv:(h + 1) * dv].astype(F32)) * h_tilde
            out_ref[bi, :, h * dv:(h + 1) * dv] = _rmsnorm(gated, gain_ref[:, h * dv:(h + 1) * dv]).astype(BF16)


def _mlstm_mixer(main, gates, out_gain, *, heads, qk_w, v_w, chunk, rows):
    b, s, _ = main.shape
    assert 2 * qk_w == v_w and s % chunk == 0 and b % rows == 0 and 2 * heads <= LANE
    dk, dv = qk_w // heads, v_w // heads
    tril3 = _cumsum_matrix(chunk)
    const = lambda i, j: (0, 0)
    return pl.pallas_call(
        functools.partial(_mlstm_kernel, heads=heads, dk=dk, dv=dv, chunk=chunk),
        grid=(b // rows, s // chunk),
        in_specs=[
            pl.BlockSpec((rows, chunk, qk_w), lambda i, j: (i, j, 0)),
            pl.BlockSpec((rows, chunk, qk_w), lambda i, j: (i, j, 1)),
            pl.BlockSpec((rows, chunk, v_w), lambda i, j: (i, j, 1)),
            pl.BlockSpec((rows, chunk, v_w), lambda i, j: (i, j, 2)),
            pl.BlockSpec((rows, chunk, LANE), lambda i, j: (i, j, 0)),
            pl.BlockSpec((1, v_w), const),
            pl.BlockSpec(tril3.shape, const),
        ],
        out_specs=pl.BlockSpec((rows, chunk, v_w), lambda i, j: (i, j, 0)),
        out_shape=jax.ShapeDtypeStruct((b, s, v_w), BF16),
        scratch_shapes=[
            pltpu.VMEM((rows, heads, dk, dv), F32),
            pltpu.VMEM((rows, heads, 1, dk), F32),
            pltpu.VMEM((rows, 1, LANE), F32),
        ],
        compiler_params=pltpu.CompilerParams(
            dimension_semantics=("parallel", "arbitrary"), vmem_limit_bytes=VMEM_LIMIT_BYTES),
        name="mlstm_mixer",
    )(main, main, main, main, gates, out_gain, tril3)


def _gla_levels(chunk, direct):
    levels, w = [], direct
    while w < chunk:
        levels.append(w)
        w *= 2
    return tuple(levels)


def _gla_segment_matrix(chunk, direct):
    j = np.arange(chunk)[:, None]
    s = np.arange(chunk)[None, :]
    blocks = [s <= j]
    for w in _gla_levels(chunk, direct):
        mid = (j // (2 * w)) * (2 * w) + w - 1
        upper = (j % (2 * w)) >= w
        blocks.append(np.where(upper, (s > mid) & (s <= j), (s > j) & (s <= mid)))
    seg = np.concatenate(blocks, axis=0).astype(np.float32)
    return jnp.asarray(np.concatenate([seg, seg], axis=1), dtype=BF16)


def _gla_kernel(q_ref, k_ref, v_ref, r_ref, zl_ref, wup_ref, bup_ref, gain_ref, seg_ref, out_ref, s_ref,
                *, heads, dk, dv, chunk, direct):
    L = chunk
    levels = _gla_levels(L, direct)

    @pl.when(pl.program_id(1) == 0)
    def _():
        s_ref[...] = jnp.zeros_like(s_ref)

    row = lax.broadcasted_iota(jnp.int32, (L, L), 0)
    col = lax.broadcasted_iota(jnp.int32, (L, L), 1)
    diff = row ^ col
    level_idx = sum(jnp.where(diff >= w, 1, 0) for w in levels)
    code = jnp.where(col > row, -1, jnp.where(diff < direct, row - col, direct - 1 + level_idx))
    pair_masks = [code == c for c in range(direct + len(levels))]
    row_d = lax.broadcasted_iota(jnp.int32, (L, dk), 0)
    upper_masks = [(row_d & w) != 0 for w in levels]

    for bi in range(q_ref.shape[0]):
        z = _dot(zl_ref[bi].astype(BF16), wup_ref[...]) + bup_ref[...]
        la = _log_sigmoid(z) * (LOG2_E / GLA_TAU)
        la_hi = la.astype(BF16)
        la_lo = (la - la_hi.astype(F32)).astype(BF16)
        seg = _dot(seg_ref[...], jnp.concatenate([la_hi, la_lo], axis=0))

        for h in range(heads):
            lanes = slice(h * dk, (h + 1) * dk)
            b = seg[0:L, lanes]
            g_tot = b[L - 1:L, :]
            qf = q_ref[bi, :, lanes].astype(F32)
            kf = k_ref[bi, :, lanes].astype(F32)
            v = v_ref[bi, :, h * dv:(h + 1) * dv]
            s_prev = s_ref[bi, h]

            o = _dot((qf * jnp.exp2(b)).astype(BF16), s_prev.astype(BF16))

            a_mat = jnp.zeros((L, L), F32)
            b3 = b.reshape(L // SUBLANE, SUBLANE, dk)
            q3 = qf.reshape(L // SUBLANE, SUBLANE, dk)
            k3 = kf.reshape(L // SUBLANE, SUBLANE, dk)
            for t in range(direct):
                if t == 0:
                    x = q3 * k3
                else:
                    x = q3 * pltpu.roll(k3, t, 1) * jnp.exp2(b3 - pltpu.roll(b3, t, 1))
                r = jnp.sum(x, axis=-1, keepdims=True).reshape(L, 1)
                a_mat = jnp.where(pair_masks[t], r, a_mat)

            for i, w in enumerate(levels):
                decay = jnp.exp2(seg[(1 + i) * L:(2 + i) * L, lanes])
                mixed = (jnp.where(upper_masks[i], qf, kf) * decay).astype(BF16)
                a_mat = jnp.where(pair_masks[direct + i], _dot_nt(mixed, mixed), a_mat)

            o = o + _dot(a_mat.astype(BF16), v)

            kg = (kf * jnp.exp2(g_tot - b)).astype(BF16)
            g_col = jnp.broadcast_to(jnp.exp2(g_tot), (LANE, dk)).T
            g_col = jnp.concatenate([g_col] * (dv // LANE), axis=1)
            s_ref[bi, h] = g_col * s_prev + _dot_tn(kg, v)

            y = _rmsnorm(o, gain_ref[:, h * dv:(h + 1) * dv])
            rr = r_ref[bi, :, h * dv:(h + 1) * dv].astype(F32)
            out_ref[bi, :, h * dv:(h + 1) * dv] = (rr * jax.nn.sigmoid(rr) * y).astype(BF16)


def _gla_mixer(main, z_low, w_up, b_up, out_gain, *, heads, qk_w, v_w, chunk, direct, rows):
    b, s, _ = main.shape
    assert 2 * qk_w == v_w and s % chunk == 0 and SUBLANE % direct == 0 and b % rows == 0
    dk, dv = qk_w // heads, v_w // heads
    seg_matrix = _gla_segment_matrix(chunk, direct)
    const = lambda i, j: (0, 0)
    return pl.pallas_call(
        functools.partial(_gla_kernel, heads=heads, dk=dk, dv=dv, chunk=chunk, direct=direct),
        grid=(b // rows, s // chunk),
        in_specs=[
            pl.BlockSpec((rows, chunk, qk_w), lambda i, j: (i, j, 0)),
            pl.BlockSpec((rows, chunk, qk_w), lambda i, j: (i, j, 1)),
            pl.BlockSpec((rows, chunk, v_w), lambda i, j: (i, j, 1)),
            pl.BlockSpec((rows, chunk, v_w), lambda i, j: (i, j, 2)),
            pl.BlockSpec((rows, chunk, LANE), lambda i, j: (i, j, 0)),
            pl.BlockSpec((LANE, qk_w), const),
            pl.BlockSpec((1, qk_w), const),
            pl.BlockSpec((1, v_w), const),
            pl.BlockSpec(seg_matrix.shape, const),
        ],
        out_specs=pl.BlockSpec((rows, chunk, v_w), lambda i, j: (i, j, 0)),
        out_shape=jax.ShapeDtypeStruct((b, s, v_w), BF16),
        scratch_shapes=[pltpu.VMEM((rows, heads, dk, dv), F32)],
        compiler_params=pltpu.CompilerParams(
            dimension_semantics=("parallel", "arbitrary"), vmem_limit_bytes=VMEM_LIMIT_BYTES),
        name="gla_mixer",
    )(main, main, main, main, z_low, w_up, b_up, out_gain, seg_matrix)


def _out_ffn_kernel(h_ref, x_ref, wo_ref, g_ref, w1_ref, w2_ref, fg_ref, o_ref, *, ff_chunk, final_norm):
    x1 = x_ref[...] + _dot(h_ref[...], wo_ref[...])
    hn = _rmsnorm(x1, g_ref[...]).astype(BF16)
    acc = x1
    for c in range(w1_ref.shape[1] // ff_chunk):
        u = jnp.maximum(_dot(hn, w1_ref[:, c * ff_chunk:(c + 1) * ff_chunk]), 0.0)
        acc = acc + _dot((u * u).astype(BF16), w2_ref[c * ff_chunk:(c + 1) * ff_chunk, :])
    if final_norm:
        acc = _rmsnorm(acc, fg_ref[...])
    o_ref[...] = acc


def _out_ffn(h2d, x2d, w_out, norm_g, w1, w2, final_g, *, final_norm, tm, ff_chunk):
    t, d = x2d.shape
    v_w, d_ff = h2d.shape[1], w1.shape[1]
    const = lambda i: (0, 0)
    resident = pl.Buffered(1)
    return pl.pallas_call(
        functools.partial(_out_ffn_kernel, ff_chunk=ff_chunk, final_norm=final_norm),
        grid=(t // tm,),
        in_specs=[
            pl.BlockSpec((tm, v_w), lambda i: (i, 0)),
            pl.BlockSpec((tm, d), lambda i: (i, 0)),
            pl.BlockSpec((v_w, d), const, pipeline_mode=resident),
            pl.BlockSpec((1, d), const),
            pl.BlockSpec((d, d_ff), const, pipeline_mode=resident),
            pl.BlockSpec((d_ff, d), const, pipeline_mode=resident),
            pl.BlockSpec((1, d), const),
        ],
        out_specs=pl.BlockSpec((tm, d), lambda i: (i, 0)),
        out_shape=jax.ShapeDtypeStruct((t, d), F32),
        compiler_params=pltpu.CompilerParams(
            dimension_semantics=("parallel",), vmem_limit_bytes=VMEM_LIMIT_BYTES),
        name="out_proj_ffn",
    )(h2d, x2d, w_out, norm_g, w1, w2, final_g)


PROJ_ROWS = 512
FFN_ROWS = 512
FFN_CHUNK = 1024
MLSTM_CHUNK = 256
GLA_CHUNK = 128
MIXER_ROWS = 2
GLA_DIRECT = 8


def _pad_lanes(a):
    return jnp.pad(a, ((0, 0), (0, LANE - a.shape[1])))


def kernel(x, norm_mix_g, norm_ffn_g, final_norm_g, mlstm_w_in, mlstm_b_gate, mlstm_out_norm_g, mlstm_w_out,
           gla_w_in, gla_w_gate_up, gla_b_gate, gla_out_norm_g, gla_w_out, ffn_w1, ffn_w2):
    b, s, d = x.shape
    depth = norm_mix_g.shape[0]
    heads = N_HEADS
    qk_w, v_w = d // 2, d
    n_main = 2 * qk_w + 2 * v_w
    dk = qk_w // heads
    t = b * s
    x2d = x.reshape(t, d)
    final_g = final_norm_g.reshape(1, d)
    n_cols = n_main // qk_w

    for i in range(depth):
        j = i // 2
        norm_g = norm_mix_g[i].reshape(1, d)
        if i % 2 == 0:
            w_in = mlstm_w_in[j]
            scales = (1.0, dk ** -0.5) + (1.0,) * (n_cols - 2)
            main, gates = _input_projection(
                x2d, norm_g, w_in[:, :n_main].astype(BF16), _pad_lanes(w_in[:, n_main:]).astype(BF16),
                _pad_lanes(mlstm_b_gate[j].reshape(1, -1)), scales, qk_w, PROJ_ROWS)
            mixed = _mlstm_mixer(main.reshape(b, s, n_main), gates.reshape(b, s, LANE),
                                 mlstm_out_norm_g[j].reshape(1, v_w),
                                 heads=heads, qk_w=qk_w, v_w=v_w, chunk=MLSTM_CHUNK, rows=MIXER_ROWS)
            w_out = mlstm_w_out[j]
        else:
            w_in = gla_w_in[j]
            scales = (dk ** -0.5,) + (1.0,) * (n_cols - 1)
            main, z_low = _input_projection(
                x2d, norm_g, w_in[:, :n_main].astype(BF16), _pad_lanes(w_in[:, n_main:]).astype(BF16),
                jnp.zeros((1, LANE), F32), scales, qk_w, PROJ_ROWS)
            rank = gla_w_gate_up.shape[1]
            w_up = jnp.pad(gla_w_gate_up[j], ((0, LANE - rank), (0, 0))).astype(BF16)
            mixed = _gla_mixer(main.reshape(b, s, n_main), z_low.reshape(b, s, LANE), w_up,
                               gla_b_gate[j].reshape(1, qk_w), gla_out_norm_g[j].reshape(1, v_w),
                               heads=heads, qk_w=qk_w, v_w=v_w, chunk=GLA_CHUNK, direct=GLA_DIRECT,
                               rows=MIXER_ROWS)
            w_out = gla_w_out[j]
        x2d = _out_ffn(mixed.reshape(t, v_w), x2d, w_out.astype(BF16), norm_ffn_g[i].reshape(1, d),
                       ffn_w1[i].astype(BF16), ffn_w2[i].astype(BF16), final_g,
                       final_norm=(i == depth - 1), tm=FFN_ROWS, ff_chunk=FFN_CHUNK)
    return x2d.reshape(b, s, d)
```

```python
import functools

import math

import numpy as np

import jax
import jax.numpy as jnp
from jax import lax
from jax.experimental import pallas as pl
from jax.experimental.pallas import tpu as pltpu

EPS = 1e-6
N_HEADS = 4
GLA_TAU = 16.0
LOG2_E = math.log2(math.e)

LANE = 128
SUBLANE = 8
VMEM_LIMIT_BYTES = 56 * 1024 * 1024

F32 = jnp.float32
BF16 = jnp.bfloat16

_NT = (((1,), (1,)), ((), ()))
_TN = (((0,), (0,)), ((), ()))


def _dot(a, b):
    return jnp.dot(a, b, preferred_element_type=F32)


def _dot_nt(a, b):
    return lax.dot_general(a, b, _NT, preferred_element_type=F32)


def _dot_tn(a, b):
    return lax.dot_general(a, b, _TN, preferred_element_type=F32)


def _rmsnorm(x, g):
    return x * lax.rsqrt(jnp.mean(x * x, axis=-1, keepdims=True) + EPS) * g


def _log_sigmoid(x):
    return jnp.minimum(x, 0.0) - jnp.log1p(jnp.exp(-jnp.abs(x)))


def _cumsum_matrix(chunk):
    tril = np.tril(np.ones((chunk, chunk), np.float32))
    return jnp.asarray(np.concatenate([tril] * 3, axis=1), dtype=BF16)


def _cumsum_rows(tril3, x):
    hi = x.astype(BF16)
    r1 = x - hi.astype(F32)
    mid = r1.astype(BF16)
    lo = (r1 - mid.astype(F32)).astype(BF16)
    return _dot(tril3, jnp.concatenate([hi, mid, lo], axis=0))


def _proj_kernel(x_ref, g_ref, w_ref, wg_ref, bg_ref, main_ref, gate_ref, *, col_scales, tn):
    xb = _rmsnorm(x_ref[...], g_ref[...]).astype(BF16)
    for c, scale in enumerate(col_scales):
        acc = _dot(xb, w_ref[:, c * tn:(c + 1) * tn])
        if scale != 1.0:
            acc = acc * scale
        main_ref[:, c * tn:(c + 1) * tn] = acc.astype(BF16)
    gate_ref[...] = _dot(xb, wg_ref[...]) + bg_ref[...]


def _input_projection(x2d, norm_g, w_main, w_gate, b_gate, col_scales, tn, tm):
    t, d = x2d.shape
    n_main = w_main.shape[1]
    const = lambda i: (0, 0)
    return pl.pallas_call(
        functools.partial(_proj_kernel, col_scales=col_scales, tn=tn),
        grid=(t // tm,),
        in_specs=[
            pl.BlockSpec((tm, d), lambda i: (i, 0)),
            pl.BlockSpec((1, d), const),
            pl.BlockSpec((d, n_main), const),
            pl.BlockSpec((d, LANE), const),
            pl.BlockSpec((1, LANE), const),
        ],
        out_specs=[
            pl.BlockSpec((tm, n_main), lambda i: (i, 0)),
            pl.BlockSpec((tm, LANE), lambda i: (i, 0)),
        ],
        out_shape=[
            jax.ShapeDtypeStruct((t, n_main), BF16),
            jax.ShapeDtypeStruct((t, LANE), F32),
        ],
        compiler_params=pltpu.CompilerParams(
            dimension_semantics=("parallel",), vmem_limit_bytes=VMEM_LIMIT_BYTES),
        name="input_projection",
    )(x2d, norm_g, w_main, w_gate, b_gate)


def _cummax_rows(x):
    L = x.shape[0]
    tiles = L // SUBLANE
    x3 = x.reshape(tiles, SUBLANE, LANE)
    sub = lax.broadcasted_iota(jnp.int32, x3.shape, 1)
    shift = 1
    while shift < SUBLANE:
        x3 = jnp.maximum(x3, jnp.where(sub >= shift, pltpu.roll(x3, shift, 1), -jnp.inf))
        shift *= 2
    tot = jnp.broadcast_to(x3[:, SUBLANE - 1:SUBLANE, :], x3.shape)
    before = jnp.concatenate([jnp.full((1, SUBLANE, LANE), -jnp.inf, F32), tot[:tiles - 1]], axis=0)
    shift = 1
    while shift < tiles:
        pad = jnp.full((shift, SUBLANE, LANE), -jnp.inf, F32)
        before = jnp.maximum(before, jnp.concatenate([pad, before[:tiles - shift]], axis=0))
        shift *= 2
    return jnp.maximum(x3, before).reshape(L, LANE)


def _mlstm_kernel(q_ref, k_ref, v_ref, o_ref, gt_ref, gain_ref, tril_ref, out_ref, c_ref, m_ref,
                  *, heads, dk, dv, chunk):
    L = chunk

    @pl.when(pl.program_id(1) == 0)
    def _():
        c_ref[...] = jnp.zeros_like(c_ref)
        m_ref[...] = jnp.zeros_like(m_ref)

    causal = lax.broadcasted_iota(jnp.int32, (L, L), 1) <= lax.broadcasted_iota(jnp.int32, (L, L), 0)
    lane = lax.broadcasted_iota(jnp.int32, (L, LANE), 1)
    head_lanes = lane < heads
    ones_col = [jnp.where(lane == h, 1.0, 0.0).astype(BF16) for h in range(heads)]

    for bi in range(q_ref.shape[0]):
        gates = gt_ref[bi]
        pre = jnp.where(head_lanes, gates, _log_sigmoid(gates)) * LOG2_E
        cum = _cumsum_rows(tril_ref[...], pre)
        rows_t = jnp.where(head_lanes, pre, cum).T
        li = jnp.where(head_lanes, pre, 0.0)
        b = jnp.where(head_lanes, pltpu.roll(cum, LANE - heads, 1), 0.0)
        m_prev = m_ref[bi]
        g_tot = b[L - 1:L, :]

        mm = jnp.maximum(m_prev, _cummax_rows(li - b))
        w_inter = jnp.exp2(m_prev - mm)

        a = g_tot - b + li
        m_new = jnp.maximum(g_tot + m_prev, jnp.max(a, axis=0, keepdims=True))
        w_s = jnp.exp2(a - m_new)
        decay = jnp.exp2(g_tot + m_prev - m_new)
        m_ref[bi] = m_new

        nums, den = [], jnp.zeros((L, LANE), F32)
        for h in range(heads):
            q = q_ref[bi, :, h * dk:(h + 1) * dk]
            k = k_ref[bi, :, h * dk:(h + 1) * dk]
            v_ext = jnp.concatenate([v_ref[bi, :, h * dv:(h + 1) * dv], ones_col[h]], axis=1)
            c_row = rows_t[h:h + 1, :] - rows_t[heads + h:heads + h + 1, :]
            p = jnp.where(causal, jnp.exp2(c_row - mm[:, h:h + 1]), 0.0)
            s = (_dot_nt(q, k) * p).astype(BF16)
            q_w = (q.astype(F32) * w_inter[:, h:h + 1]).astype(BF16)
            state = c_ref[bi, h]
            num = _dot(q_w, state.astype(BF16)) + _dot(s, v_ext)
            nums.append(num[:, :dv])
            den = den + num[:, dv:]

            kw = (k.astype(F32) * w_s[:, h:h + 1]).astype(BF16)
            c_ref[bi, h] = decay[:, h:h + 1] * state + _dot_tn(kw, v_ext)

        inv = 1.0 / jnp.maximum(jnp.abs(den), jnp.exp2(-(b + mm)))
        for h in range(heads):
            gated = jax.nn.sigmoid(o_ref[bi, :, h * dv:(h + 1) * dv].astype(F32)) * (nums[h] * inv[:, h:h + 1])
            out_ref[bi, :, h * dv:(h + 1) * dv] = _rmsnorm(gated, gain_ref[:, h * dv:(h + 1) * dv]).astype(BF16)


def _mlstm_mixer(main, gates, out_gain, *, heads, qk_w, v_w, chunk, rows):
    b, s, _ = main.shape
    assert 2 * qk_w == v_w and s % chunk == 0 and b % rows == 0 and 2 * heads <= LANE
    dk, dv = qk_w // heads, v_w // heads
    tril3 = _cumsum_matrix(chunk)
    const = lambda i, j: (0, 0)
    return pl.pallas_call(
        functools.partial(_mlstm_kernel, heads=heads, dk=dk, dv=dv, chunk=chunk),
        grid=(b // rows, s // chunk),
        in_specs=[
            pl.BlockSpec((rows, chunk, qk_w), lambda i, j: (i, j, 0)),
            pl.BlockSpec((rows, chunk, qk_w), lambda i, j: (i, j, 1)),
            pl.BlockSpec((rows, chunk, v_w), lambda i, j: (i, j, 1)),
            pl.BlockSpec((rows, chunk, v_w), lambda i, j: (i, j, 2)),
            pl.BlockSpec((rows, chunk, LANE), lambda i, j: (i, j, 0)),
            pl.BlockSpec((1, v_w), const),
            pl.BlockSpec(tril3.shape, const),
        ],
        out_specs=pl.BlockSpec((rows, chunk, v_w), lambda i, j: (i, j, 0)),
        out_shape=jax.ShapeDtypeStruct((b, s, v_w), BF16),
        scratch_shapes=[
            pltpu.VMEM((rows, heads, dk, dv + LANE), F32),
            pltpu.VMEM((rows, 1, LANE), F32),
        ],
        compiler_params=pltpu.CompilerParams(
            dimension_semantics=("parallel", "arbitrary"), vmem_limit_bytes=VMEM_LIMIT_BYTES),
        name="mlstm_mixer",
    )(main, main, main, main, gates, out_gain, tril3)


def _gla_levels(chunk, direct):
    levels, w = [], direct
    while w < chunk:
        levels.append(w)
        w *= 2
    return tuple(levels)


def _gla_segment_matrix(chunk, direct):
    j = np.arange(chunk)[:, None]
    s = np.arange(chunk)[None, :]
    blocks = [s <= j]
    for w in _gla_levels(chunk, direct):
        mid = (j // (2 * w)) * (2 * w) + w - 1
        upper = (j % (2 * w)) >= w
        blocks.append(np.where(upper, (s > mid) & (s <= j), (s > j) & (s <= mid)))
    seg = np.concatenate(blocks, axis=0).astype(np.float32)
    return jnp.asarray(np.concatenate([seg, seg], axis=1), dtype=BF16)


def _gla_kernel(q_ref, k_ref, v_ref, r_ref, zl_ref, wup_ref, bup_ref, gain_ref, seg_ref, out_ref, s_ref,
                *, heads, dk, dv, chunk, direct):
    L = chunk
    levels = _gla_levels(L, direct)

    @pl.when(pl.program_id(1) == 0)
    def _():
        s_ref[...] = jnp.zeros_like(s_ref)

    row = lax.broadcasted_iota(jnp.int32, (L, L), 0)
    col = lax.broadcasted_iota(jnp.int32, (L, L), 1)
    diff = row ^ col
    level_idx = sum(jnp.where(diff >= w, 1, 0) for w in levels)
    code = jnp.where(col > row, -1, jnp.where(diff < direct, row - col, direct - 1 + level_idx))
    pair_masks = [code == c for c in range(direct + len(levels))]
    row_d = lax.broadcasted_iota(jnp.int32, (L, dk), 0)
    upper_masks = [(row_d & w) != 0 for w in levels]

    for bi in range(q_ref.shape[0]):
        z = _dot(zl_ref[bi].astype(BF16), wup_ref[...]) + bup_ref[...]
        la = _log_sigmoid(z) * (LOG2_E / GLA_TAU)
        la_hi = la.astype(BF16)
        la_lo = (la - la_hi.astype(F32)).astype(BF16)
        seg = _dot(seg_ref[...], jnp.concatenate([la_hi, la_lo], axis=0))

        for h in range(heads):
            lanes = slice(h * dk, (h + 1) * dk)
            b = seg[0:L, lanes]
            g_tot = b[L - 1:L, :]
            qf = q_ref[bi, :, lanes].astype(F32)
            kf = k_ref[bi, :, lanes].astype(F32)
            v = v_ref[bi, :, h * dv:(h + 1) * dv]
            s_prev = s_ref[bi, h]

            o = _dot((qf * jnp.exp2(b)).astype(BF16), s_prev.astype(BF16))

            a_mat = jnp.zeros((L, L), F32)
            b3 = b.reshape(L // SUBLANE, SUBLANE, dk)
            q3 = qf.reshape(L // SUBLANE, SUBLANE, dk)
            k3 = kf.reshape(L // SUBLANE, SUBLANE, dk)
            for t in range(direct):
                if t == 0:
                    x = q3 * k3
                else:
                    x = q3 * pltpu.roll(k3, t, 1) * jnp.exp2(b3 - pltpu.roll(b3, t, 1))
                r = jnp.sum(x, axis=-1, keepdims=True).reshape(L, 1)
                a_mat = jnp.where(pair_masks[t], r, a_mat)

            for i, w in enumerate(levels):
                decay = jnp.exp2(seg[(1 + i) * L:(2 + i) * L, lanes])
                mixed = (jnp.where(upper_masks[i], qf, kf) * decay).astype(BF16)
                a_mat = jnp.where(pair_masks[direct + i], _dot_nt(mixed, mixed), a_mat)

            o = o + _dot(a_mat.astype(BF16), v)

            kg = (kf * jnp.exp2(g_tot - b)).astype(BF16)
            g_col = jnp.broadcast_to(jnp.exp2(g_tot), (LANE, dk)).T
            g_col = jnp.concatenate([g_col] * (dv // LANE), axis=1)
            s_ref[bi, h] = g_col * s_prev + _dot_tn(kg, v)

            y = _rmsnorm(o, gain_ref[:, h * dv:(h + 1) * dv])
            rr = r_ref[bi, :, h * dv:(h + 1) * dv].astype(F32)
            out_ref[bi, :, h * dv:(h + 1) * dv] = (rr * jax.nn.sigmoid(rr) * y).astype(BF16)


def _gla_mixer(main, z_low, w_up, b_up, out_gain, *, heads, qk_w, v_w, chunk, direct, rows):
    b, s, _ = main.shape
    assert 2 * qk_w == v_w and s % chunk == 0 and SUBLANE % direct == 0 and b % rows == 0
    dk, dv = qk_w // heads, v_w // heads
    seg_matrix = _gla_segment_matrix(chunk, direct)
    const = lambda i, j: (0, 0)
    return pl.pallas_call(
        functools.partial(_gla_kernel, heads=heads, dk=dk, dv=dv, chunk=chunk, direct=direct),
        grid=(b // rows, s // chunk),
        in_specs=[
            pl.BlockSpec((rows, chunk, qk_w), lambda i, j: (i, j, 0)),
            pl.BlockSpec((rows, chunk, qk_w), lambda i, j: (i, j, 1)),
            pl.BlockSpec((rows, chunk, v_w), lambda i, j: (i, j, 1)),
            pl.BlockSpec((rows, chunk, v_w), lambda i, j: (i, j, 2)),
            pl.BlockSpec((rows, chunk, LANE), lambda i, j: (i, j, 0)),
            pl.BlockSpec((LANE, qk_w), const),
            pl.BlockSpec((1, qk_w), const),
            pl.BlockSpec((1, v_w), const),
            pl.BlockSpec(seg_matrix.shape, const),
        ],
        out_specs=pl.BlockSpec((rows, chunk, v_w), lambda i, j: (i, j, 0)),
        out_shape=jax.ShapeDtypeStruct((b, s, v_w), BF16),
        scratch_shapes=[pltpu.VMEM((rows, heads, dk, dv), F32)],
        compiler_params=pltpu.CompilerParams(
            dimension_semantics=("parallel", "arbitrary"), vmem_limit_bytes=VMEM_LIMIT_BYTES),
        name="gla_mixer",
    )(main, main, main, main, z_low, w_up, b_up, out_gain, seg_matrix)


def _out_ffn_kernel(h_ref, x_ref, wo_ref, g_ref, w1_ref, w2_ref, fg_ref, o_ref, *, ff_chunk, final_norm):
    x1 = x_ref[...] + _dot(h_ref[...], wo_ref[...])
    hn = _rmsnorm(x1, g_ref[...]).astype(BF16)
    acc = x1
    for c in range(w1_ref.shape[1] // ff_chunk):
        u = jnp.maximum(_dot(hn, w1_ref[:, c * ff_chunk:(c + 1) * ff_chunk]), 0.0)
        acc = acc + _dot((u * u).astype(BF16), w2_ref[c * ff_chunk:(c + 1) * ff_chunk, :])
    if final_norm:
        acc = _rmsnorm(acc, fg_ref[...])
    o_ref[...] = acc


def _out_ffn(h2d, x2d, w_out, norm_g, w1, w2, final_g, *, final_norm, tm, ff_chunk):
    t, d = x2d.shape
    v_w, d_ff = h2d.shape[1], w1.shape[1]
    const = lambda i: (0, 0)
    resident = pl.Buffered(1)
    return pl.pallas_call(
        functools.partial(_out_ffn_kernel, ff_chunk=ff_chunk, final_norm=final_norm),
        grid=(t // tm,),
        in_specs=[
            pl.BlockSpec((tm, v_w), lambda i: (i, 0)),
            pl.BlockSpec((tm, d), lambda i: (i, 0)),
            pl.BlockSpec((v_w, d), const, pipeline_mode=resident),
            pl.BlockSpec((1, d), const),
            pl.BlockSpec((d, d_ff), const, pipeline_mode=resident),
            pl.BlockSpec((d_ff, d), const, pipeline_mode=resident),
            pl.BlockSpec((1, d), const),
        ],
        out_specs=pl.BlockSpec((tm, d), lambda i: (i, 0)),
        out_shape=jax.ShapeDtypeStruct((t, d), F32),
        compiler_params=pltpu.CompilerParams(
            dimension_semantics=("parallel",), vmem_limit_bytes=VMEM_LIMIT_BYTES),
        name="out_proj_ffn",
    )(h2d, x2d, w_out, norm_g, w1, w2, final_g)


PROJ_ROWS = 512
FFN_ROWS = 512
FFN_CHUNK = 1024
MLSTM_CHUNK = 256
GLA_CHUNK = 128
MIXER_ROWS = 4
GLA_DIRECT = 8


def _pad_lanes(a):
    return jnp.pad(a, ((0, 0), (0, LANE - a.shape[1])))


def kernel(x, norm_mix_g, norm_ffn_g, final_norm_g, mlstm_w_in, mlstm_b_gate, mlstm_out_norm_g, mlstm_w_out,
           gla_w_in, gla_w_gate_up, gla_b_gate, gla_out_norm_g, gla_w_out, ffn_w1, ffn_w2):
    b, s, d = x.shape
    depth = norm_mix_g.shape[0]
    heads = N_HEADS
    qk_w, v_w = d // 2, d
    n_main = 2 * qk_w + 2 * v_w
    dk = qk_w // heads
    t = b * s
    x2d = x.reshape(t, d)
    final_g = final_norm_g.reshape(1, d)
    n_cols = n_main // qk_w

    for i in range(depth):
        j = i // 2
        norm_g = norm_mix_g[i].reshape(1, d)
        if i % 2 == 0:
            w_in = mlstm_w_in[j]
            scales = (1.0, dk ** -0.5) + (1.0,) * (n_cols - 2)
            main, gates = _input_projection(
                x2d, norm_g, w_in[:, :n_main].astype(BF16), _pad_lanes(w_in[:, n_main:]).astype(BF16),
                _pad_lanes(mlstm_b_gate[j].reshape(1, -1)), scales, qk_w, PROJ_ROWS)
            mixed = _mlstm_mixer(main.reshape(b, s, n_main), gates.reshape(b, s, LANE),
                                 mlstm_out_norm_g[j].reshape(1, v_w),
                                 heads=heads, qk_w=qk_w, v_w=v_w, chunk=MLSTM_CHUNK, rows=MIXER_ROWS)
            w_out = mlstm_w_out[j]
        else:
            w_in = gla_w_in[j]
            scales = (dk ** -0.5,) + (1.0,) * (n_cols - 1)
            main, z_low = _input_projection(
                x2d, norm_g, w_in[:, :n_main].astype(BF16), _pad_lanes(w_in[:, n_main:]).astype(BF16),
                jnp.zeros((1, LANE), F32), scales, qk_w, PROJ_ROWS)
            rank = gla_w_gate_up.shape[1]
            w_up = jnp.pad(gla_w_gate_up[j], ((0, LANE - rank), (0, 0))).astype(BF16)
            mixed = _gla_mixer(main.reshape(b, s, n_main), z_low.reshape(b, s, LANE), w_up,
                               gla_b_gate[j].reshape(1, qk_w), gla_out_norm_g[j].reshape(1, v_w),
                               heads=heads, qk_w=qk_w, v_w=v_w, chunk=GLA_CHUNK, direct=GLA_DIRECT,
                               rows=MIXER_ROWS)
            w_out = gla_w_out[j]
        x2d = _out_ffn(mixed.reshape(t, v_w), x2d, w_out.astype(BF16), norm_ffn_g[i].reshape(1, d),
                       ffn_w1[i].astype(BF16), ffn_w2[i].astype(BF16), final_g,
                       final_norm=(i == depth - 1), tm=FFN_ROWS, ff_chunk=FFN_CHUNK)
    return x2d.reshape(b, s, d)
```

```python
import functools

import math

import numpy as np

import jax
import jax.numpy as jnp
from jax import lax
from jax.experimental import pallas as pl
from jax.experimental.pallas import tpu as pltpu

EPS = 1e-6
N_HEADS = 4
GLA_TAU = 16.0
LOG2_E = math.log2(math.e)

LANE = 128
SUBLANE = 8
VMEM_LIMIT_BYTES = 56 * 1024 * 1024

F32 = jnp.float32
BF16 = jnp.bfloat16

_NT = (((1,), (1,)), ((), ()))
_TN = (((0,), (0,)), ((), ()))


def _dot(a, b):
    return jnp.dot(a, b, preferred_element_type=F32)


def _dot_nt(a, b):
    return lax.dot_general(a, b, _NT, preferred_element_type=F32)


def _dot_tn(a, b):
    return lax.dot_general(a, b, _TN, preferred_element_type=F32)


def _rmsnorm(x, g):
    return x * lax.rsqrt(jnp.mean(x * x, axis=-1, keepdims=True) + EPS) * g


def _log_sigmoid(x):
    return jnp.minimum(x, 0.0) - jnp.log(1.0 + jnp.exp(-jnp.abs(x)))


def _cumsum_matrix(chunk):
    tril = np.tril(np.ones((chunk, chunk), np.float32))
    return jnp.asarray(np.concatenate([tril] * 3, axis=1), dtype=BF16)


def _cumsum_rows(tril3, x):
    hi = x.astype(BF16)
    r1 = x - hi.astype(F32)
    mid = r1.astype(BF16)
    lo = (r1 - mid.astype(F32)).astype(BF16)
    return _dot(tril3, jnp.concatenate([hi, mid, lo], axis=0))


def _proj_kernel(x_ref, g_ref, w_ref, wg_ref, bg_ref, main_ref, gate_ref, *, col_scales, tn):
    xb = _rmsnorm(x_ref[...], g_ref[...]).astype(BF16)
    for c, scale in enumerate(col_scales):
        acc = _dot(xb, w_ref[:, c * tn:(c + 1) * tn])
        if scale != 1.0:
            acc = acc * scale
        main_ref[:, c * tn:(c + 1) * tn] = acc.astype(BF16)
    gate_ref[...] = _dot(xb, wg_ref[...]) + bg_ref[...]


def _input_projection(x2d, norm_g, w_main, w_gate, b_gate, col_scales, tn, tm):
    t, d = x2d.shape
    n_main = w_main.shape[1]
    const = lambda i: (0, 0)
    return pl.pallas_call(
        functools.partial(_proj_kernel, col_scales=col_scales, tn=tn),
        grid=(t // tm,),
        in_specs=[
            pl.BlockSpec((tm, d), lambda i: (i, 0)),
            pl.BlockSpec((1, d), const),
            pl.BlockSpec((d, n_main), const),
            pl.BlockSpec((d, LANE), const),
            pl.BlockSpec((1, LANE), const),
        ],
        out_specs=[
            pl.BlockSpec((tm, n_main), lambda i: (i, 0)),
            pl.BlockSpec((tm, LANE), lambda i: (i, 0)),
        ],
        out_shape=[
            jax.ShapeDtypeStruct((t, n_main), BF16),
            jax.ShapeDtypeStruct((t, LANE), F32),
        ],
        compiler_params=pltpu.CompilerParams(
            dimension_semantics=("parallel",), vmem_limit_bytes=VMEM_LIMIT_BYTES),
        name="input_projection",
    )(x2d, norm_g, w_main, w_gate, b_gate)


def _cummax_rows(x):
    L = x.shape[0]
    tiles = L // SUBLANE
    x3 = x.reshape(tiles, SUBLANE, LANE)
    sub = lax.broadcasted_iota(jnp.int32, x3.shape, 1)
    shift = 1
    while shift < SUBLANE:
        x3 = jnp.maximum(x3, jnp.where(sub >= shift, pltpu.roll(x3, shift, 1), -jnp.inf))
        shift *= 2
    tot = jnp.broadcast_to(x3[:, SUBLANE - 1:SUBLANE, :], x3.shape)
    before = jnp.concatenate([jnp.full((1, SUBLANE, LANE), -jnp.inf, F32), tot[:tiles - 1]], axis=0)
    shift = 1
    while shift < tiles:
        pad = jnp.full((shift, SUBLANE, LANE), -jnp.inf, F32)
        before = jnp.maximum(before, jnp.concatenate([pad, before[:tiles - shift]], axis=0))
        shift *= 2
    return jnp.maximum(x3, before).reshape(L, LANE)


def _mlstm_kernel(q_ref, k_ref, v_ref, o_ref, gt_ref, gain_ref, tril_ref, out_ref, c_ref, m_ref,
                  *, heads, dk, dv, chunk):
    L = chunk

    @pl.when(pl.program_id(1) == 0)
    def _():
        c_ref[...] = jnp.zeros_like(c_ref)
        m_ref[...] = jnp.zeros_like(m_ref)

    causal = lax.broadcasted_iota(jnp.int32, (L, L), 1) <= lax.broadcasted_iota(jnp.int32, (L, L), 0)
    lane = lax.broadcasted_iota(jnp.int32, (L, LANE), 1)
    head_lanes = lane < heads
    ones_col = [jnp.where(lane == h, 1.0, 0.0).astype(BF16) for h in range(heads)]

    for bi in range(q_ref.shape[0]):
        gates = gt_ref[bi]
        pre = jnp.where(head_lanes, gates, _log_sigmoid(gates)) * LOG2_E
        cum = _cumsum_rows(tril_ref[...], pre)
        rows_t = jnp.where(head_lanes, pre, cum).T
        li = jnp.where(head_lanes, pre, 0.0)
        b = jnp.where(head_lanes, pltpu.roll(cum, LANE - heads, 1), 0.0)
        m_prev = m_ref[bi]
        g_tot = b[L - 1:L, :]

        mm = jnp.maximum(m_prev, _cummax_rows(li - b))
        w_inter = jnp.exp2(m_prev - mm)

        a = g_tot - b + li
        m_new = jnp.maximum(g_tot + m_prev, jnp.max(a, axis=0, keepdims=True))
        w_s = jnp.exp2(a - m_new)
        decay = jnp.exp2(g_tot + m_prev - m_new)
        m_ref[bi] = m_new

        nums, den = [], jnp.zeros((L, LANE), F32)
        for h in range(heads):
            q = q_ref[bi, :, h * dk:(h + 1) * dk]
            k = k_ref[bi, :, h * dk:(h + 1) * dk]
            v_ext = jnp.concatenate([v_ref[bi, :, h * dv:(h + 1) * dv], ones_col[h]], axis=1)
            c_row = rows_t[h:h + 1, :] - rows_t[heads + h:heads + h + 1, :]
            p = jnp.where(causal, jnp.exp2(c_row - mm[:, h:h + 1]), 0.0)
            s = (_dot_nt(q, k) * p).astype(BF16)
            q_w = (q.astype(F32) * w_inter[:, h:h + 1]).astype(BF16)
            state = c_ref[bi, h]
            num = _dot(q_w, state.astype(BF16)) + _dot(s, v_ext)
            nums.append(num[:, :dv])
            den = den + num[:, dv:]

            kw = (k.astype(F32) * w_s[:, h:h + 1]).astype(BF16)
            c_ref[bi, h] = decay[:, h:h + 1] * state + _dot_tn(kw, v_ext)

        inv = 1.0 / jnp.maximum(jnp.abs(den), jnp.exp2(-(b + mm)))
        for h in range(heads):
            gated = jax.nn.sigmoid(o_ref[bi, :, h * dv:(h + 1) * dv].astype(F32)) * (nums[h] * inv[:, h:h + 1])
            out_ref[bi, :, h * dv:(h + 1) * dv] = _rmsnorm(gated, gain_ref[:, h * dv:(h + 1) * dv]).astype(BF16)


def _mlstm_mixer(main, gates, out_gain, *, heads, qk_w, v_w, chunk, rows):
    b, s, _ = main.shape
    assert 2 * qk_w == v_w and s % chunk == 0 and b % rows == 0 and 2 * heads <= LANE
    dk, dv = qk_w // heads, v_w // heads
    tril3 = _cumsum_matrix(chunk)
    const = lambda i, j: (0, 0)
    return pl.pallas_call(
        functools.partial(_mlstm_kernel, heads=heads, dk=dk, dv=dv, chunk=chunk),
        grid=(b // rows, s // chunk),
        in_specs=[
            pl.BlockSpec((rows, chunk, qk_w), lambda i, j: (i, j, 0)),
            pl.BlockSpec((rows, chunk, qk_w), lambda i, j: (i, j, 1)),
            pl.BlockSpec((rows, chunk, v_w), lambda i, j: (i, j, 1)),
            pl.BlockSpec((rows, chunk, v_w), lambda i, j: (i, j, 2)),
            pl.BlockSpec((rows, chunk, LANE), lambda i, j: (i, j, 0)),
            pl.BlockSpec((1, v_w), const),
            pl.BlockSpec(tril3.shape, const),
        ],
        out_specs=pl.BlockSpec((rows, chunk, v_w), lambda i, j: (i, j, 0)),
        out_shape=jax.ShapeDtypeStruct((b, s, v_w), BF16),
        scratch_shapes=[
            pltpu.VMEM((rows, heads, dk, dv + LANE), F32),
            pltpu.VMEM((rows, 1, LANE), F32),
        ],
        compiler_params=pltpu.CompilerParams(
            dimension_semantics=("parallel", "arbitrary"), vmem_limit_bytes=VMEM_LIMIT_BYTES),
        name="mlstm_mixer",
    )(main, main, main, main, gates, out_gain, tril3)


def _gla_levels(chunk, direct):
    levels, w = [], direct
    while w < chunk:
        levels.append(w)
        w *= 2
    return tuple(levels)


def _gla_segment_matrix(chunk, direct):
    j = np.arange(chunk)[:, None]
    s = np.arange(chunk)[None, :]
    blocks = [s <= j]
    for w in _gla_levels(chunk, direct):
        mid = (j // (2 * w)) * (2 * w) + w - 1
        upper = (j % (2 * w)) >= w
        blocks.append(np.where(upper, (s > mid) & (s <= j), (s > j) & (s <= mid)))
    seg = np.concatenate(blocks, axis=0).astype(np.float32)
    return jnp.asarray(np.concatenate([seg, seg], axis=1), dtype=BF16)


def _gla_kernel(q_ref, k_ref, v_ref, r_ref, zl_ref, wup_ref, bup_ref, gain_ref, seg_ref, out_ref, s_ref,
                *, heads, dk, dv, chunk, direct):
    L = chunk
    levels = _gla_levels(L, direct)

    @pl.when(pl.program_id(1) == 0)
    def _():
        s_ref[...] = jnp.zeros_like(s_ref)

    row = lax.broadcasted_iota(jnp.int32, (L, L), 0)
    col = lax.broadcasted_iota(jnp.int32, (L, L), 1)
    diff = row ^ col
    level_idx = sum(jnp.where(diff >= w, 1, 0) for w in levels)
    code = jnp.where(col > row, -1, jnp.where(diff < direct, row - col, direct - 1 + level_idx))
    pair_masks = [code == c for c in range(direct + len(levels))]
    row_d = lax.broadcasted_iota(jnp.int32, (L, dk), 0)
    upper_masks = [(row_d & w) != 0 for w in levels]

    for bi in range(q_ref.shape[0]):
        z = _dot(zl_ref[bi].astype(BF16), wup_ref[...]) + bup_ref[...]
        la = _log_sigmoid(z) * (LOG2_E / GLA_TAU)
        step = jnp.exp2(la)
        la_hi = la.astype(BF16)
        la_lo = (la - la_hi.astype(F32)).astype(BF16)
        seg = _dot(seg_ref[...], jnp.concatenate([la_hi, la_lo], axis=0))

        for h in range(heads):
            lanes = slice(h * dk, (h + 1) * dk)
            b = seg[0:L, lanes]
            g_tot = b[L - 1:L, :]
            qf = q_ref[bi, :, lanes].astype(F32)
            kf = k_ref[bi, :, lanes].astype(F32)
            v = v_ref[bi, :, h * dv:(h + 1) * dv]
            s_prev = s_ref[bi, h]

            o = _dot((qf * jnp.exp2(b)).astype(BF16), s_prev.astype(BF16))

            a_mat = jnp.zeros((L, L), F32)
            q3 = qf.reshape(L // SUBLANE, SUBLANE, dk)
            step3 = step[:, lanes].reshape(L // SUBLANE, SUBLANE, dk)
            y = kf.reshape(L // SUBLANE, SUBLANE, dk)
            for t in range(direct):
                if t > 0:
                    y = step3 * pltpu.roll(y, 1, 1)
                r = jnp.sum(q3 * y, axis=-1, keepdims=True).reshape(L, 1)
                a_mat = jnp.where(pair_masks[t], r, a_mat)

            for i, w in enumerate(levels):
                decay = jnp.exp2(seg[(1 + i) * L:(2 + i) * L, lanes])
                mixed = (jnp.where(upper_masks[i], qf, kf) * decay).astype(BF16)
                a_mat = jnp.where(pair_masks[direct + i], _dot_nt(mixed, mixed), a_mat)

            o = o + _dot(a_mat.astype(BF16), v)

            kg = (kf * jnp.exp2(g_tot - b)).astype(BF16)
            g_col = jnp.broadcast_to(jnp.exp2(g_tot), (LANE, dk)).T
            g_col = jnp.concatenate([g_col] * (dv // LANE), axis=1)
            s_ref[bi, h] = g_col * s_prev + _dot_tn(kg, v)

            y = _rmsnorm(o, gain_ref[:, h * dv:(h + 1) * dv])
            rr = r_ref[bi, :, h * dv:(h + 1) * dv].astype(F32)
            out_ref[bi, :, h * dv:(h + 1) * dv] = (rr * jax.nn.sigmoid(rr) * y).astype(BF16)


def _gla_mixer(main, z_low, w_up, b_up, out_gain, *, heads, qk_w, v_w, chunk, direct, rows):
    b, s, _ = main.shape
    assert 2 * qk_w == v_w and s % chunk == 0 and SUBLANE % direct == 0 and b % rows == 0
    dk, dv = qk_w // heads, v_w // heads
    seg_matrix = _gla_segment_matrix(chunk, direct)
    const = lambda i, j: (0, 0)
    return pl.pallas_call(
        functools.partial(_gla_kernel, heads=heads, dk=dk, dv=dv, chunk=chunk, direct=direct),
        grid=(b // rows, s // chunk),
        in_specs=[
            pl.BlockSpec((rows, chunk, qk_w), lambda i, j: (i, j, 0)),
            pl.BlockSpec((rows, chunk, qk_w), lambda i, j: (i, j, 1)),
            pl.BlockSpec((rows, chunk, v_w), lambda i, j: (i, j, 1)),
            pl.BlockSpec((rows, chunk, v_w), lambda i, j: (i, j, 2)),
            pl.BlockSpec((rows, chunk, LANE), lambda i, j: (i, j, 0)),
            pl.BlockSpec((LANE, qk_w), const),
            pl.BlockSpec((1, qk_w), const),
            pl.BlockSpec((1, v_w), const),
            pl.BlockSpec(seg_matrix.shape, const),
        ],
        out_specs=pl.BlockSpec((rows, chunk, v_w), lambda i, j: (i, j, 0)),
        out_shape=jax.ShapeDtypeStruct((b, s, v_w), BF16),
        scratch_shapes=[pltpu.VMEM((rows, heads, dk, dv), F32)],
        compiler_params=pltpu.CompilerParams(
            dimension_semantics=("parallel", "arbitrary"), vmem_limit_bytes=VMEM_LIMIT_BYTES),
        name="gla_mixer",
    )(main, main, main, main, z_low, w_up, b_up, out_gain, seg_matrix)


def _out_ffn_kernel(h_ref, x_ref, wo_ref, g_ref, w1_ref, w2_ref, fg_ref, o_ref, *, ff_chunk, final_norm):
    x1 = x_ref[...] + _dot(h_ref[...], wo_ref[...])
    hn = _rmsnorm(x1, g_ref[...]).astype(BF16)
    acc = x1
    for c in range(w1_ref.shape[1] // ff_chunk):
        u = jnp.maximum(_dot(hn, w1_ref[:, c * ff_chunk:(c + 1) * ff_chunk]), 0.0)
        acc = acc + _dot((u * u).astype(BF16), w2_ref[c * ff_chunk:(c + 1) * ff_chunk, :])
    if final_norm:
        acc = _rmsnorm(acc, fg_ref[...])
    o_ref[...] = acc


def _out_ffn(h2d, x2d, w_out, norm_g, w1, w2, final_g, *, final_norm, tm, ff_chunk):
    t, d = x2d.shape
    v_w, d_ff = h2d.shape[1], w1.shape[1]
    const = lambda i: (0, 0)
    resident = pl.Buffered(1)
    return pl.pallas_call(
        functools.partial(_out_ffn_kernel, ff_chunk=ff_chunk, final_norm=final_norm),
        grid=(t // tm,),
        in_specs=[
            pl.BlockSpec((tm, v_w), lambda i: (i, 0)),
            pl.BlockSpec((tm, d), lambda i: (i, 0)),
            pl.BlockSpec((v_w, d), const, pipeline_mode=resident),
            pl.BlockSpec((1, d), const),
            pl.BlockSpec((d, d_ff), const, pipeline_mode=resident),
            pl.BlockSpec((d_ff, d), const, pipeline_mode=resident),
            pl.BlockSpec((1, d), const),
        ],
        out_specs=pl.BlockSpec((tm, d), lambda i: (i, 0)),
        out_shape=jax.ShapeDtypeStruct((t, d), F32),
        compiler_params=pltpu.CompilerParams(
            dimension_semantics=("parallel",), vmem_limit_bytes=VMEM_LIMIT_BYTES),
        name="out_proj_ffn",
    )(h2d, x2d, w_out, norm_g, w1, w2, final_g)


PROJ_ROWS = 512
FFN_ROWS = 512
FFN_CHUNK = 1024
MLSTM_CHUNK = 256
GLA_CHUNK = 128
MIXER_ROWS = 4
GLA_DIRECT = 8


def _pad_lanes(a):
    return jnp.pad(a, ((0, 0), (0, LANE - a.shape[1])))


def kernel(x, norm_mix_g, norm_ffn_g, final_norm_g, mlstm_w_in, mlstm_b_gate, mlstm_out_norm_g, mlstm_w_out,
           gla_w_in, gla_w_gate_up, gla_b_gate, gla_out_norm_g, gla_w_out, ffn_w1, ffn_w2):
    b, s, d = x.shape
    depth = norm_mix_g.shape[0]
    heads = N_HEADS
    qk_w, v_w = d // 2, d
    n_main = 2 * qk_w + 2 * v_w
    dk = qk_w // heads
    t = b * s
    x2d = x.reshape(t, d)
    final_g = final_norm_g.reshape(1, d)
    n_cols = n_main // qk_w

    for i in range(depth):
        j = i // 2
        norm_g = norm_mix_g[i].reshape(1, d)
        if i % 2 == 0:
            w_in = mlstm_w_in[j]
            scales = (1.0, dk ** -0.5) + (1.0,) * (n_cols - 2)
            main, gates = _input_projection(
                x2d, norm_g, w_in[:, :n_main].astype(BF16), _pad_lanes(w_in[:, n_main:]).astype(BF16),
                _pad_lanes(mlstm_b_gate[j].reshape(1, -1)), scales, qk_w, PROJ_ROWS)
            mixed = _mlstm_mixer(main.reshape(b, s, n_main), gates.reshape(b, s, LANE),
                                 mlstm_out_norm_g[j].reshape(1, v_w),
                                 heads=heads, qk_w=qk_w, v_w=v_w, chunk=MLSTM_CHUNK, rows=MIXER_ROWS)
            w_out = mlstm_w_out[j]
        else:
            w_in = gla_w_in[j]
            scales = (dk ** -0.5,) + (1.0,) * (n_cols - 1)
            main, z_low = _input_projection(
                x2d, norm_g, w_in[:, :n_main].astype(BF16), _pad_lanes(w_in[:, n_main:]).astype(BF16),
                jnp.zeros((1, LANE), F32), scales, qk_w, PROJ_ROWS)
            rank = gla_w_gate_up.shape[1]
            w_up = jnp.pad(gla_w_gate_up[j], ((0, LANE - rank), (0, 0))).astype(BF16)
            mixed = _gla_mixer(main.reshape(b, s, n_main), z_low.reshape(b, s, LANE), w_up,
                               gla_b_gate[j].reshape(1, qk_w), gla_out_norm_g[j].reshape(1, v_w),
                               heads=heads, qk_w=qk_w, v_w=v_w, chunk=GLA_CHUNK, direct=GLA_DIRECT,
                               rows=MIXER_ROWS)
            w_out = gla_w_out[j]
        x2d = _out_ffn(mixed.reshape(t, v_w), x2d, w_out.astype(BF16), norm_ffn_g[i].reshape(1, d),
                       ffn_w1[i].astype(BF16), ffn_w2[i].astype(BF16), final_g,
                       final_norm=(i == depth - 1), tm=FFN_ROWS, ff_chunk=FFN_CHUNK)
    return x2d.reshape(b, s, d)
```

```python
import functools

import math

import numpy as np

import jax
import jax.numpy as jnp
from jax import lax
from jax.experimental import pallas as pl
from jax.experimental.pallas import tpu as pltpu

EPS = 1e-6
N_HEADS = 4
GLA_TAU = 16.0
LOG2_E = math.log2(math.e)

LANE = 128
SUBLANE = 8
VMEM_LIMIT_BYTES = 56 * 1024 * 1024

F32 = jnp.float32
BF16 = jnp.bfloat16

_NT = (((1,), (1,)), ((), ()))
_TN = (((0,), (0,)), ((), ()))


def _dot(a, b):
    return jnp.dot(a, b, preferred_element_type=F32)


def _dot_nt(a, b):
    return lax.dot_general(a, b, _NT, preferred_element_type=F32)


def _dot_tn(a, b):
    return lax.dot_general(a, b, _TN, preferred_element_type=F32)


def _rmsnorm(x, g):
    return x * lax.rsqrt(jnp.mean(x * x, axis=-1, keepdims=True) + EPS) * g


def _log_sigmoid(x):
    return jnp.minimum(x, 0.0) - jnp.log(1.0 + jnp.exp(-jnp.abs(x)))


def _cumsum_matrix(chunk):
    tril = np.tril(np.ones((chunk, chunk), np.float32))
    return jnp.asarray(np.concatenate([tril] * 3, axis=1), dtype=BF16)


def _cumsum_rows(tril3, x):
    hi = x.astype(BF16)
    r1 = x - hi.astype(F32)
    mid = r1.astype(BF16)
    lo = (r1 - mid.astype(F32)).astype(BF16)
    return _dot(tril3, jnp.concatenate([hi, mid, lo], axis=0))


def _proj_kernel(x_ref, g_ref, w_ref, wg_ref, bg_ref, main_ref, gate_ref, *, col_scales, tn):
    xb = _rmsnorm(x_ref[...], g_ref[...]).astype(BF16)
    for c, scale in enumerate(col_scales):
        acc = _dot(xb, w_ref[:, c * tn:(c + 1) * tn])
        if scale != 1.0:
            acc = acc * scale
        main_ref[:, c * tn:(c + 1) * tn] = acc.astype(BF16)
    gate_ref[...] = _dot(xb, wg_ref[...]) + bg_ref[...]


def _input_projection(x2d, norm_g, w_in_all, layer, n_main, w_gate, b_gate, col_scales, tn, tm):
    t, d = x2d.shape
    const = lambda i: (0, 0)
    return pl.pallas_call(
        functools.partial(_proj_kernel, col_scales=col_scales, tn=tn),
        grid=(t // tm,),
        in_specs=[
            pl.BlockSpec((tm, d), lambda i: (i, 0)),
            pl.BlockSpec((1, d), const),
            pl.BlockSpec((None, d, n_main), lambda i: (layer, 0, 0)),
            pl.BlockSpec((d, LANE), const),
            pl.BlockSpec((1, LANE), const),
        ],
        out_specs=[
            pl.BlockSpec((tm, n_main), lambda i: (i, 0)),
            pl.BlockSpec((tm, LANE), lambda i: (i, 0)),
        ],
        out_shape=[
            jax.ShapeDtypeStruct((t, n_main), BF16),
            jax.ShapeDtypeStruct((t, LANE), F32),
        ],
        compiler_params=pltpu.CompilerParams(
            dimension_semantics=("parallel",), vmem_limit_bytes=VMEM_LIMIT_BYTES),
        name="input_projection",
    )(x2d, norm_g, w_in_all, w_gate, b_gate)


def _cummax_rows(x):
    L = x.shape[0]
    tiles = L // SUBLANE
    x3 = x.reshape(tiles, SUBLANE, LANE)
    sub = lax.broadcasted_iota(jnp.int32, x3.shape, 1)
    shift = 1
    while shift < SUBLANE:
        x3 = jnp.maximum(x3, jnp.where(sub >= shift, pltpu.roll(x3, shift, 1), -jnp.inf))
        shift *= 2
    tot = jnp.broadcast_to(x3[:, SUBLANE - 1:SUBLANE, :], x3.shape)
    before = jnp.concatenate([jnp.full((1, SUBLANE, LANE), -jnp.inf, F32), tot[:tiles - 1]], axis=0)
    shift = 1
    while shift < tiles:
        pad = jnp.full((shift, SUBLANE, LANE), -jnp.inf, F32)
        before = jnp.maximum(before, jnp.concatenate([pad, before[:tiles - shift]], axis=0))
        shift *= 2
    return jnp.maximum(x3, before).reshape(L, LANE)


def _mlstm_kernel(q_ref, k_ref, v_ref, o_ref, gt_ref, gain_ref, tril_ref, out_ref, c_ref, m_ref,
                  *, heads, dk, dv, chunk):
    L = chunk

    @pl.when(pl.program_id(1) == 0)
    def _():
        c_ref[...] = jnp.zeros_like(c_ref)
        m_ref[...] = jnp.zeros_like(m_ref)

    causal = lax.broadcasted_iota(jnp.int32, (L, L), 1) <= lax.broadcasted_iota(jnp.int32, (L, L), 0)
    lane = lax.broadcasted_iota(jnp.int32, (L, LANE), 1)
    head_lanes = lane < heads
    ones_col = [jnp.where(lane == h, 1.0, 0.0).astype(BF16) for h in range(heads)]

    for bi in range(q_ref.shape[0]):
        gates = gt_ref[bi]
        pre = jnp.where(head_lanes, gates, _log_sigmoid(gates)) * LOG2_E
        cum = _cumsum_rows(tril_ref[...], pre)
        rows_t = jnp.where(head_lanes, pre, cum).T
        li = jnp.where(head_lanes, pre, 0.0)
        b = jnp.where(head_lanes, pltpu.roll(cum, LANE - heads, 1), 0.0)
        m_prev = m_ref[bi]
        g_tot = b[L - 1:L, :]

        mm = jnp.maximum(m_prev, _cummax_rows(li - b))
        w_inter = jnp.exp2(m_prev - mm)

        a = g_tot - b + li
        m_new = jnp.maximum(g_tot + m_prev, jnp.max(a, axis=0, keepdims=True))
        w_s = jnp.exp2(a - m_new)
        decay = jnp.exp2(g_tot + m_prev - m_new)
        m_ref[bi] = m_new

        nums, den = [], jnp.zeros((L, LANE), F32)
        for h in range(heads):
            q = q_ref[bi, :, h * dk:(h + 1) * dk]
            k = k_ref[bi, :, h * dk:(h + 1) * dk]
            v_ext = jnp.concatenate([v_ref[bi, :, h * dv:(h + 1) * dv], ones_col[h]], axis=1)
            c_row = rows_t[h:h + 1, :] - rows_t[heads + h:heads + h + 1, :]
            p = jnp.where(causal, jnp.exp2(c_row - mm[:, h:h + 1]), 0.0)
            s = (_dot_nt(q, k) * p).astype(BF16)
            q_w = (q.astype(F32) * w_inter[:, h:h + 1]).astype(BF16)
            state = c_ref[bi, h]
            num = _dot(q_w, state.astype(BF16)) + _dot(s, v_ext)
            nums.append(num[:, :dv])
            den = den + num[:, dv:]

            kw = (k.astype(F32) * w_s[:, h:h + 1]).astype(BF16)
            c_ref[bi, h] = decay[:, h:h + 1] * state + _dot_tn(kw, v_ext)

        inv = 1.0 / jnp.maximum(jnp.abs(den), jnp.exp2(-(b + mm)))
        for h in range(heads):
            gated = jax.nn.sigmoid(o_ref[bi, :, h * dv:(h + 1) * dv].astype(F32)) * (nums[h] * inv[:, h:h + 1])
            out_ref[bi, :, h * dv:(h + 1) * dv] = _rmsnorm(gated, gain_ref[:, h * dv:(h + 1) * dv]).astype(BF16)


def _mlstm_mixer(main, gates, out_gain, *, heads, qk_w, v_w, chunk, rows):
    b, s, _ = main.shape
    assert 2 * qk_w == v_w and s % chunk == 0 and b % rows == 0 and 2 * heads <= LANE
    dk, dv = qk_w // heads, v_w // heads
    tril3 = _cumsum_matrix(chunk)
    const = lambda i, j: (0, 0)
    return pl.pallas_call(
        functools.partial(_mlstm_kernel, heads=heads, dk=dk, dv=dv, chunk=chunk),
        grid=(b // rows, s // chunk),
        in_specs=[
            pl.BlockSpec((rows, chunk, qk_w), lambda i, j: (i, j, 0)),
            pl.BlockSpec((rows, chunk, qk_w), lambda i, j: (i, j, 1)),
            pl.BlockSpec((rows, chunk, v_w), lambda i, j: (i, j, 1)),
            pl.BlockSpec((rows, chunk, v_w), lambda i, j: (i, j, 2)),
            pl.BlockSpec((rows, chunk, LANE), lambda i, j: (i, j, 0)),
            pl.BlockSpec((1, v_w), const),
            pl.BlockSpec(tril3.shape, const),
        ],
        out_specs=pl.BlockSpec((rows, chunk, v_w), lambda i, j: (i, j, 0)),
        out_shape=jax.ShapeDtypeStruct((b, s, v_w), BF16),
        scratch_shapes=[
            pltpu.VMEM((rows, heads, dk, dv + LANE), F32),
            pltpu.VMEM((rows, 1, LANE), F32),
        ],
        compiler_params=pltpu.CompilerParams(
            dimension_semantics=("parallel", "arbitrary"), vmem_limit_bytes=VMEM_LIMIT_BYTES),
        name="mlstm_mixer",
    )(main, main, main, main, gates, out_gain, tril3)


def _gla_levels(chunk, direct):
    levels, w = [], direct
    while w < chunk:
        levels.append(w)
        w *= 2
    return tuple(levels)


def _gla_segment_matrix(chunk, direct):
    j = np.arange(chunk)[:, None]
    s = np.arange(chunk)[None, :]
    blocks = [s <= j]
    for w in _gla_levels(chunk, direct):
        mid = (j // (2 * w)) * (2 * w) + w - 1
        upper = (j % (2 * w)) >= w
        blocks.append(np.where(upper, (s > mid) & (s <= j), (s > j) & (s <= mid)))
    seg = np.concatenate(blocks, axis=0).astype(np.float32)
    return jnp.asarray(np.concatenate([seg, seg], axis=1), dtype=BF16)


def _gla_kernel(q_ref, k_ref, v_ref, r_ref, zl_ref, wup_ref, bup_ref, gain_ref, seg_ref, out_ref, s_ref,
                *, heads, dk, dv, chunk, direct):
    L = chunk
    levels = _gla_levels(L, direct)

    @pl.when(pl.program_id(1) == 0)
    def _():
        s_ref[...] = jnp.zeros_like(s_ref)

    row = lax.broadcasted_iota(jnp.int32, (L, L), 0)
    col = lax.broadcasted_iota(jnp.int32, (L, L), 1)
    diff = row ^ col
    level_idx = sum(jnp.where(diff >= w, 1, 0) for w in levels)
    code = jnp.where(col > row, -1, jnp.where(diff < direct, row - col, direct - 1 + level_idx))
    pair_masks = [code == c for c in range(direct + len(levels))]
    row_d = lax.broadcasted_iota(jnp.int32, (L, dk), 0)
    upper_masks = [(row_d & w) != 0 for w in levels]

    for bi in range(q_ref.shape[0]):
        z = _dot(zl_ref[bi].astype(BF16), wup_ref[...]) + bup_ref[...]
        la = _log_sigmoid(z) * (LOG2_E / GLA_TAU)
        step = jnp.exp2(la)
        la_hi = la.astype(BF16)
        la_lo = (la - la_hi.astype(F32)).astype(BF16)
        seg = _dot(seg_ref[...], jnp.concatenate([la_hi, la_lo], axis=0))

        for h in range(heads):
            lanes = slice(h * dk, (h + 1) * dk)
            b = seg[0:L, lanes]
            g_tot = b[L - 1:L, :]
            qf = q_ref[bi, :, lanes].astype(F32)
            kf = k_ref[bi, :, lanes].astype(F32)
            v = v_ref[bi, :, h * dv:(h + 1) * dv]
            s_prev = s_ref[bi, h]

            o = _dot((qf * jnp.exp2(b)).astype(BF16), s_prev.astype(BF16))

            a_mat = jnp.zeros((L, L), F32)
            q3 = qf.reshape(L // SUBLANE, SUBLANE, dk)
            step3 = step[:, lanes].reshape(L // SUBLANE, SUBLANE, dk)
            y = kf.reshape(L // SUBLANE, SUBLANE, dk)
            for t in range(direct):
                if t > 0:
                    y = step3 * pltpu.roll(y, 1, 1)
                r = jnp.sum(q3 * y, axis=-1, keepdims=True).reshape(L, 1)
                a_mat = jnp.where(pair_masks[t], r, a_mat)

            for i, w in enumerate(levels):
                decay = jnp.exp2(seg[(1 + i) * L:(2 + i) * L, lanes])
                mixed = (jnp.where(upper_masks[i], qf, kf) * decay).astype(BF16)
                a_mat = jnp.where(pair_masks[direct + i], _dot_nt(mixed, mixed), a_mat)

            o = o + _dot(a_mat.astype(BF16), v)

            kg = (kf * jnp.exp2(g_tot - b)).astype(BF16)
            g_col = jnp.broadcast_to(jnp.exp2(g_tot), (LANE, dk)).T
            g_col = jnp.concatenate([g_col] * (dv // LANE), axis=1)
            s_ref[bi, h] = g_col * s_prev + _dot_tn(kg, v)

            y = _rmsnorm(o, gain_ref[:, h * dv:(h + 1) * dv])
            rr = r_ref[bi, :, h * dv:(h + 1) * dv].astype(F32)
            out_ref[bi, :, h * dv:(h + 1) * dv] = (rr * jax.nn.sigmoid(rr) * y).astype(BF16)


def _gla_mixer(main, z_low, w_up, b_up, out_gain, *, heads, qk_w, v_w, chunk, direct, rows):
    b, s, _ = main.shape
    assert 2 * qk_w == v_w and s % chunk == 0 and SUBLANE % direct == 0 and b % rows == 0
    dk, dv = qk_w // heads, v_w // heads
    seg_matrix = _gla_segment_matrix(chunk, direct)
    const = lambda i, j: (0, 0)
    return pl.pallas_call(
        functools.partial(_gla_kernel, heads=heads, dk=dk, dv=dv, chunk=chunk, direct=direct),
        grid=(b // rows, s // chunk),
        in_specs=[
            pl.BlockSpec((rows, chunk, qk_w), lambda i, j: (i, j, 0)),
            pl.BlockSpec((rows, chunk, qk_w), lambda i, j: (i, j, 1)),
            pl.BlockSpec((rows, chunk, v_w), lambda i, j: (i, j, 1)),
            pl.BlockSpec((rows, chunk, v_w), lambda i, j: (i, j, 2)),
            pl.BlockSpec((rows, chunk, LANE), lambda i, j: (i, j, 0)),
            pl.BlockSpec((LANE, qk_w), const),
            pl.BlockSpec((1, qk_w), const),
            pl.BlockSpec((1, v_w), const),
            pl.BlockSpec(seg_matrix.shape, const),
        ],
        out_specs=pl.BlockSpec((rows, chunk, v_w), lambda i, j: (i, j, 0)),
        out_shape=jax.ShapeDtypeStruct((b, s, v_w), BF16),
        scratch_shapes=[pltpu.VMEM((rows, heads, dk, dv), F32)],
        compiler_params=pltpu.CompilerParams(
            dimension_semantics=("parallel", "arbitrary"), vmem_limit_bytes=VMEM_LIMIT_BYTES),
        name="gla_mixer",
    )(main, main, main, main, z_low, w_up, b_up, out_gain, seg_matrix)


def _out_ffn_kernel(h_ref, x_ref, wo_ref, g_ref, w1_ref, w2_ref, fg_ref, o_ref, *, ff_chunk, final_norm):
    x1 = x_ref[...] + _dot(h_ref[...], wo_ref[...])
    hn = _rmsnorm(x1, g_ref[...]).astype(BF16)
    acc = x1
    for c in range(w1_ref.shape[1] // ff_chunk):
        u = jnp.maximum(_dot(hn, w1_ref[:, c * ff_chunk:(c + 1) * ff_chunk]), 0.0)
        acc = acc + _dot((u * u).astype(BF16), w2_ref[c * ff_chunk:(c + 1) * ff_chunk, :])
    if final_norm:
        acc = _rmsnorm(acc, fg_ref[...])
    o_ref[...] = acc


def _out_ffn(h2d, x2d, w_out_all, mixer_layer, norm_g, w1_all, w2_all, layer, final_g,
             *, final_norm, tm, ff_chunk):
    t, d = x2d.shape
    v_w, d_ff = h2d.shape[1], w1_all.shape[2]
    const = lambda i: (0, 0)
    resident = pl.Buffered(1)
    return pl.pallas_call(
        functools.partial(_out_ffn_kernel, ff_chunk=ff_chunk, final_norm=final_norm),
        grid=(t // tm,),
        in_specs=[
            pl.BlockSpec((tm, v_w), lambda i: (i, 0)),
            pl.BlockSpec((tm, d), lambda i: (i, 0)),
            pl.BlockSpec((None, v_w, d), lambda i: (mixer_layer, 0, 0), pipeline_mode=resident),
            pl.BlockSpec((1, d), const),
            pl.BlockSpec((None, d, d_ff), lambda i: (layer, 0, 0), pipeline_mode=resident),
            pl.BlockSpec((None, d_ff, d), lambda i: (layer, 0, 0), pipeline_mode=resident),
            pl.BlockSpec((1, d), const),
        ],
        out_specs=pl.BlockSpec((tm, d), lambda i: (i, 0)),
        out_shape=jax.ShapeDtypeStruct((t, d), F32),
        compiler_params=pltpu.CompilerParams(
            dimension_semantics=("parallel",), vmem_limit_bytes=VMEM_LIMIT_BYTES),
        name="out_proj_ffn",
    )(h2d, x2d, w_out_all, norm_g, w1_all, w2_all, final_g)


PROJ_ROWS = 1024
FFN_ROWS = 1024
FFN_CHUNK = 1024
MLSTM_CHUNK = 256
GLA_CHUNK = 128
MIXER_ROWS = 4
GLA_DIRECT = 8


def _pad_lanes(a):
    return jnp.pad(a, ((0, 0), (0, LANE - a.shape[1])))


def kernel(x, norm_mix_g, norm_ffn_g, final_norm_g, mlstm_w_in, mlstm_b_gate, mlstm_out_norm_g, mlstm_w_out,
           gla_w_in, gla_w_gate_up, gla_b_gate, gla_out_norm_g, gla_w_out, ffn_w1, ffn_w2):
    b, s, d = x.shape
    depth = norm_mix_g.shape[0]
    heads = N_HEADS
    qk_w, v_w = d // 2, d
    n_main = 2 * qk_w + 2 * v_w
    dk = qk_w // heads
    t = b * s
    x2d = x.reshape(t, d)
    final_g = final_norm_g.reshape(1, d)
    n_cols = n_main // qk_w

    mlstm_w_in_b, gla_w_in_b = mlstm_w_in.astype(BF16), gla_w_in.astype(BF16)
    mlstm_w_out_b, gla_w_out_b = mlstm_w_out.astype(BF16), gla_w_out.astype(BF16)
    ffn_w1_b, ffn_w2_b = ffn_w1.astype(BF16), ffn_w2.astype(BF16)

    for i in range(depth):
        j = i // 2
        norm_g = norm_mix_g[i].reshape(1, d)
        if i % 2 == 0:
            scales = (1.0, dk ** -0.5) + (1.0,) * (n_cols - 2)
            main, gates = _input_projection(
                x2d, norm_g, mlstm_w_in_b, j, n_main, _pad_lanes(mlstm_w_in[j][:, n_main:]).astype(BF16),
                _pad_lanes(mlstm_b_gate[j].reshape(1, -1)), scales, qk_w, PROJ_ROWS)
            mixed = _mlstm_mixer(main.reshape(b, s, n_main), gates.reshape(b, s, LANE),
                                 mlstm_out_norm_g[j].reshape(1, v_w),
                                 heads=heads, qk_w=qk_w, v_w=v_w, chunk=MLSTM_CHUNK, rows=MIXER_ROWS)
            w_out_b = mlstm_w_out_b
        else:
            scales = (dk ** -0.5,) + (1.0,) * (n_cols - 1)
            main, z_low = _input_projection(
                x2d, norm_g, gla_w_in_b, j, n_main, _pad_lanes(gla_w_in[j][:, n_main:]).astype(BF16),
                jnp.zeros((1, LANE), F32), scales, qk_w, PROJ_ROWS)
            rank = gla_w_gate_up.shape[1]
            w_up = jnp.pad(gla_w_gate_up[j], ((0, LANE - rank), (0, 0))).astype(BF16)
            mixed = _gla_mixer(main.reshape(b, s, n_main), z_low.reshape(b, s, LANE), w_up,
                               gla_b_gate[j].reshape(1, qk_w), gla_out_norm_g[j].reshape(1, v_w),
                               heads=heads, qk_w=qk_w, v_w=v_w, chunk=GLA_CHUNK, direct=GLA_DIRECT,
                               rows=MIXER_ROWS)
            w_out_b = gla_w_out_b
        x2d = _out_ffn(mixed.reshape(t, v_w), x2d, w_out_b, j, norm_ffn_g[i].reshape(1, d),
                       ffn_w1_b, ffn_w2_b, i, final_g,
                       final_norm=(i == depth - 1), tm=FFN_ROWS, ff_chunk=FFN_CHUNK)
    return x2d.reshape(b, s, d)
```

```python
import functools

import math

import numpy as np

import jax
import jax.numpy as jnp
from jax import lax
from jax.experimental import pallas as pl
from jax.experimental.pallas import tpu as pltpu

EPS = 1e-6
N_HEADS = 4
GLA_TAU = 16.0
LOG2_E = math.log2(math.e)

LANE = 128
SUBLANE = 8
VMEM_LIMIT_BYTES = 56 * 1024 * 1024

F32 = jnp.float32
BF16 = jnp.bfloat16

_NT = (((1,), (1,)), ((), ()))
_TN = (((0,), (0,)), ((), ()))


def _dot(a, b):
    return jnp.dot(a, b, preferred_element_type=F32)


def _dot_nt(a, b):
    return lax.dot_general(a, b, _NT, preferred_element_type=F32)


def _dot_tn(a, b):
    return lax.dot_general(a, b, _TN, preferred_element_type=F32)


def _rmsnorm(x, g):
    return x * lax.rsqrt(jnp.mean(x * x, axis=-1, keepdims=True) + EPS) * g


def _log_sigmoid(x):
    return jnp.minimum(x, 0.0) - jnp.log(1.0 + jnp.exp(-jnp.abs(x)))


def _cumsum_matrix(chunk):
    tril = np.tril(np.ones((chunk, chunk), np.float32))
    return jnp.asarray(np.concatenate([tril] * 3, axis=1), dtype=BF16)


def _cumsum_rows(tril3, x):
    hi = x.astype(BF16)
    r1 = x - hi.astype(F32)
    mid = r1.astype(BF16)
    lo = (r1 - mid.astype(F32)).astype(BF16)
    return _dot(tril3, jnp.concatenate([hi, mid, lo], axis=0))


def _proj_kernel(x_ref, g_ref, w_ref, wg_ref, bg_ref, main_ref, gate_ref, *, col_scales, tn):
    xb = _rmsnorm(x_ref[...], g_ref[...]).astype(BF16)
    for c, scale in enumerate(col_scales):
        acc = _dot(xb, w_ref[:, c * tn:(c + 1) * tn])
        if scale != 1.0:
            acc = acc * scale
        main_ref[:, c * tn:(c + 1) * tn] = acc.astype(BF16)
    gate_ref[...] = _dot(xb, wg_ref[...]) + bg_ref[...]


def _input_projection(x2d, norm_g, w_in_all, layer, n_main, w_gate, b_gate, col_scales, tn, tm):
    t, d = x2d.shape
    const = lambda i: (0, 0)
    return pl.pallas_call(
        functools.partial(_proj_kernel, col_scales=col_scales, tn=tn),
        grid=(t // tm,),
        in_specs=[
            pl.BlockSpec((tm, d), lambda i: (i, 0)),
            pl.BlockSpec((1, d), const),
            pl.BlockSpec((None, d, n_main), lambda i: (layer, 0, 0)),
            pl.BlockSpec((d, LANE), const),
            pl.BlockSpec((1, LANE), const),
        ],
        out_specs=[
            pl.BlockSpec((tm, n_main), lambda i: (i, 0)),
            pl.BlockSpec((tm, LANE), lambda i: (i, 0)),
        ],
        out_shape=[
            jax.ShapeDtypeStruct((t, n_main), BF16),
            jax.ShapeDtypeStruct((t, LANE), F32),
        ],
        compiler_params=pltpu.CompilerParams(
            dimension_semantics=("parallel",), vmem_limit_bytes=VMEM_LIMIT_BYTES),
        name="input_projection",
    )(x2d, norm_g, w_in_all, w_gate, b_gate)


def _cummax_rows(x):
    L = x.shape[0]
    tiles = L // SUBLANE
    x3 = x.reshape(tiles, SUBLANE, LANE)
    sub = lax.broadcasted_iota(jnp.int32, x3.shape, 1)
    shift = 1
    while shift < SUBLANE:
        x3 = jnp.maximum(x3, jnp.where(sub >= shift, pltpu.roll(x3, shift, 1), -jnp.inf))
        shift *= 2
    tot = jnp.broadcast_to(x3[:, SUBLANE - 1:SUBLANE, :], x3.shape)
    before = jnp.concatenate([jnp.full((1, SUBLANE, LANE), -jnp.inf, F32), tot[:tiles - 1]], axis=0)
    shift = 1
    while shift < tiles:
        pad = jnp.full((shift, SUBLANE, LANE), -jnp.inf, F32)
        before = jnp.maximum(before, jnp.concatenate([pad, before[:tiles - shift]], axis=0))
        shift *= 2
    return jnp.maximum(x3, before).reshape(L, LANE)


def _mlstm_kernel(q_ref, k_ref, v_ref, o_ref, gt_ref, gain_ref, tril_ref, out_ref, c_ref, m_ref,
                  *, heads, dk, dv, chunk):
    L = chunk

    @pl.when(pl.program_id(1) == 0)
    def _():
        c_ref[...] = jnp.zeros_like(c_ref)
        m_ref[...] = jnp.zeros_like(m_ref)

    causal = lax.broadcasted_iota(jnp.int32, (L, L), 1) <= lax.broadcasted_iota(jnp.int32, (L, L), 0)
    lane = lax.broadcasted_iota(jnp.int32, (L, LANE), 1)
    head_lanes = lane < heads
    ones_col = [jnp.where(lane == h, 1.0, 0.0).astype(BF16) for h in range(heads)]

    pending = None
    for bi in range(q_ref.shape[0]):
        gates = gt_ref[bi]
        pre = jnp.where(head_lanes, gates, _log_sigmoid(gates)) * LOG2_E
        cum = _cumsum_rows(tril_ref[...], pre)
        rows_t = jnp.where(head_lanes, pre, cum).T
        li = jnp.where(head_lanes, pre, 0.0)
        b = jnp.where(head_lanes, pltpu.roll(cum, LANE - heads, 1), 0.0)
        m_prev = m_ref[bi]
        g_tot = b[L - 1:L, :]

        mm = jnp.maximum(m_prev, _cummax_rows(li - b))
        w_inter = jnp.exp2(m_prev - mm)

        a = g_tot - b + li
        m_new = jnp.maximum(g_tot + m_prev, jnp.max(a, axis=0, keepdims=True))
        w_s = jnp.exp2(a - m_new)
        decay = jnp.exp2(g_tot + m_prev - m_new)
        m_ref[bi] = m_new

        nums, den = [], jnp.zeros((L, LANE), F32)
        for h in range(heads):
            q = q_ref[bi, :, h * dk:(h + 1) * dk]
            k = k_ref[bi, :, h * dk:(h + 1) * dk]
            v_ext = jnp.concatenate([v_ref[bi, :, h * dv:(h + 1) * dv], ones_col[h]], axis=1)
            c_row = rows_t[h:h + 1, :] - rows_t[heads + h:heads + h + 1, :]
            p = jnp.where(causal, jnp.exp2(c_row - mm[:, h:h + 1]), 0.0)
            s = (_dot_nt(q, k) * p).astype(BF16)
            q_w = (q.astype(F32) * w_inter[:, h:h + 1]).astype(BF16)
            state = c_ref[bi, h]
            num = _dot(q_w, state.astype(BF16)) + _dot(s, v_ext)
            nums.append(num[:, :dv])
            den = den + num[:, dv:]

            kw = (k.astype(F32) * w_s[:, h:h + 1]).astype(BF16)
            c_ref[bi, h] = decay[:, h:h + 1] * state + _dot_tn(kw, v_ext)

            if pending is not None:
                _mlstm_epilogue(o_ref, gain_ref, out_ref, *pending, h, dv)

        inv = 1.0 / jnp.maximum(jnp.abs(den), jnp.exp2(-(b + mm)))
        pending = (bi, nums, inv)
    for h in range(heads):
        _mlstm_epilogue(o_ref, gain_ref, out_ref, *pending, h, dv)


def _mlstm_epilogue(o_ref, gain_ref, out_ref, bi, nums, inv, h, dv):
    cols = slice(h * dv, (h + 1) * dv)
    gated = jax.nn.sigmoid(o_ref[bi, :, cols].astype(F32)) * (nums[h] * inv[:, h:h + 1])
    out_ref[bi, :, cols] = _rmsnorm(gated, gain_ref[:, cols]).astype(BF16)


def _mlstm_mixer(main, gates, out_gain, *, heads, qk_w, v_w, chunk, rows):
    b, s, _ = main.shape
    assert 2 * qk_w == v_w and s % chunk == 0 and b % rows == 0 and 2 * heads <= LANE
    dk, dv = qk_w // heads, v_w // heads
    tril3 = _cumsum_matrix(chunk)
    const = lambda i, j: (0, 0)
    return pl.pallas_call(
        functools.partial(_mlstm_kernel, heads=heads, dk=dk, dv=dv, chunk=chunk),
        grid=(b // rows, s // chunk),
        in_specs=[
            pl.BlockSpec((rows, chunk, qk_w), lambda i, j: (i, j, 0)),
            pl.BlockSpec((rows, chunk, qk_w), lambda i, j: (i, j, 1)),
            pl.BlockSpec((rows, chunk, v_w), lambda i, j: (i, j, 1)),
            pl.BlockSpec((rows, chunk, v_w), lambda i, j: (i, j, 2)),
            pl.BlockSpec((rows, chunk, LANE), lambda i, j: (i, j, 0)),
            pl.BlockSpec((1, v_w), const),
            pl.BlockSpec(tril3.shape, const),
        ],
        out_specs=pl.BlockSpec((rows, chunk, v_w), lambda i, j: (i, j, 0)),
        out_shape=jax.ShapeDtypeStruct((b, s, v_w), BF16),
        scratch_shapes=[
            pltpu.VMEM((rows, heads, dk, dv + LANE), F32),
            pltpu.VMEM((rows, 1, LANE), F32),
        ],
        compiler_params=pltpu.CompilerParams(
            dimension_semantics=("parallel", "arbitrary"), vmem_limit_bytes=VMEM_LIMIT_BYTES),
        name="mlstm_mixer",
    )(main, main, main, main, gates, out_gain, tril3)


def _gla_levels(chunk, direct):
    levels, w = [], direct
    while w < chunk:
        levels.append(w)
        w *= 2
    return tuple(levels)


def _gla_segment_matrix(chunk, direct):
    j = np.arange(chunk)[:, None]
    s = np.arange(chunk)[None, :]
    blocks = [s <= j]
    for w in _gla_levels(chunk, direct):
        mid = (j // (2 * w)) * (2 * w) + w - 1
        upper = (j % (2 * w)) >= w
        blocks.append(np.where(upper, (s > mid) & (s <= j), (s > j) & (s <= mid)))
    seg = np.concatenate(blocks, axis=0).astype(np.float32)
    return jnp.asarray(np.concatenate([seg, seg], axis=1), dtype=BF16)


def _gla_kernel(q_ref, k_ref, v_ref, r_ref, zl_ref, wup_ref, bup_ref, gain_ref, seg_ref, out_ref, s_ref,
                *, heads, dk, dv, chunk, direct):
    L = chunk
    levels = _gla_levels(L, direct)

    @pl.when(pl.program_id(1) == 0)
    def _():
        s_ref[...] = jnp.zeros_like(s_ref)

    row = lax.broadcasted_iota(jnp.int32, (L, L), 0)
    col = lax.broadcasted_iota(jnp.int32, (L, L), 1)
    diff = row ^ col
    level_idx = sum(jnp.where(diff >= w, 1, 0) for w in levels)
    code = jnp.where(col > row, -1, jnp.where(diff < direct, row - col, direct - 1 + level_idx))
    pair_masks = [code == c for c in range(direct + len(levels))]
    row_d = lax.broadcasted_iota(jnp.int32, (L, dk), 0)
    upper_masks = [(row_d & w) != 0 for w in levels]

    def decays(bi):
        z = _dot(zl_ref[bi].astype(BF16), wup_ref[...]) + bup_ref[...]
        la = _log_sigmoid(z) * (LOG2_E / GLA_TAU)
        la_hi = la.astype(BF16)
        la_lo = (la - la_hi.astype(F32)).astype(BF16)
        seg = _dot(seg_ref[...], jnp.concatenate([la_hi, la_lo], axis=0))
        return jnp.exp2(la), seg

    n_rows = q_ref.shape[0]
    pending = None
    next_decays = decays(0)
    for bi in range(n_rows):
        step, seg = next_decays
        if bi + 1 < n_rows:
            next_decays = decays(bi + 1)

        for h in range(heads):
            lanes = slice(h * dk, (h + 1) * dk)
            b = seg[0:L, lanes]
            g_tot = b[L - 1:L, :]
            qf = q_ref[bi, :, lanes].astype(F32)
            kf = k_ref[bi, :, lanes].astype(F32)
            v = v_ref[bi, :, h * dv:(h + 1) * dv]
            s_prev = s_ref[bi, h]

            o = _dot((qf * jnp.exp2(b)).astype(BF16), s_prev.astype(BF16))

            a_mat = jnp.zeros((L, L), F32)
            q3 = qf.reshape(L // SUBLANE, SUBLANE, dk)
            step3 = step[:, lanes].reshape(L // SUBLANE, SUBLANE, dk)
            y = kf.reshape(L // SUBLANE, SUBLANE, dk)
            for t in range(direct):
                if t > 0:
                    y = step3 * pltpu.roll(y, 1, 1)
                r = jnp.sum(q3 * y, axis=-1, keepdims=True).reshape(L, 1)
                a_mat = jnp.where(pair_masks[t], r, a_mat)

            for i, w in enumerate(levels):
                decay = jnp.exp2(seg[(1 + i) * L:(2 + i) * L, lanes])
                mixed = (jnp.where(upper_masks[i], qf, kf) * decay).astype(BF16)
                a_mat = jnp.where(pair_masks[direct + i], _dot_nt(mixed, mixed), a_mat)

            o = o + _dot(a_mat.astype(BF16), v)

            kg = (kf * jnp.exp2(g_tot - b)).astype(BF16)
            g_col = jnp.broadcast_to(jnp.exp2(g_tot), (LANE, dk)).T
            g_col = jnp.concatenate([g_col] * (dv // LANE), axis=1)
            s_ref[bi, h] = g_col * s_prev + _dot_tn(kg, v)

            if pending is not None:
                _gla_epilogue(r_ref, gain_ref, out_ref, *pending, dv)
            pending = (bi, h, o)
    _gla_epilogue(r_ref, gain_ref, out_ref, *pending, dv)


def _gla_epilogue(r_ref, gain_ref, out_ref, bi, h, o, dv):
    cols = slice(h * dv, (h + 1) * dv)
    rr = r_ref[bi, :, cols].astype(F32)
    out_ref[bi, :, cols] = (rr * jax.nn.sigmoid(rr) * _rmsnorm(o, gain_ref[:, cols])).astype(BF16)


def _gla_mixer(main, z_low, w_up, b_up, out_gain, *, heads, qk_w, v_w, chunk, direct, rows):
    b, s, _ = main.shape
    assert 2 * qk_w == v_w and s % chunk == 0 and SUBLANE % direct == 0 and b % rows == 0
    dk, dv = qk_w // heads, v_w // heads
    seg_matrix = _gla_segment_matrix(chunk, direct)
    const = lambda i, j: (0, 0)
    return pl.pallas_call(
        functools.partial(_gla_kernel, heads=heads, dk=dk, dv=dv, chunk=chunk, direct=direct),
        grid=(b // rows, s // chunk),
        in_specs=[
            pl.BlockSpec((rows, chunk, qk_w), lambda i, j: (i, j, 0)),
            pl.BlockSpec((rows, chunk, qk_w), lambda i, j: (i, j, 1)),
            pl.BlockSpec((rows, chunk, v_w), lambda i, j: (i, j, 1)),
            pl.BlockSpec((rows, chunk, v_w), lambda i, j: (i, j, 2)),
            pl.BlockSpec((rows, chunk, LANE), lambda i, j: (i, j, 0)),
            pl.BlockSpec((LANE, qk_w), const),
            pl.BlockSpec((1, qk_w), const),
            pl.BlockSpec((1, v_w), const),
            pl.BlockSpec(seg_matrix.shape, const),
        ],
        out_specs=pl.BlockSpec((rows, chunk, v_w), lambda i, j: (i, j, 0)),
        out_shape=jax.ShapeDtypeStruct((b, s, v_w), BF16),
        scratch_shapes=[pltpu.VMEM((rows, heads, dk, dv), F32)],
        compiler_params=pltpu.CompilerParams(
            dimension_semantics=("parallel", "arbitrary"), vmem_limit_bytes=VMEM_LIMIT_BYTES),
        name="gla_mixer",
    )(main, main, main, main, z_low, w_up, b_up, out_gain, seg_matrix)


def _out_ffn_kernel(h_ref, x_ref, wo_ref, g_ref, w1_ref, w2_ref, fg_ref, o_ref, *, ff_chunk, final_norm):
    x1 = x_ref[...] + _dot(h_ref[...], wo_ref[...])
    hn = _rmsnorm(x1, g_ref[...]).astype(BF16)
    acc = x1
    for c in range(w1_ref.shape[1] // ff_chunk):
        u = jnp.maximum(_dot(hn, w1_ref[:, c * ff_chunk:(c + 1) * ff_chunk]), 0.0)
        acc = acc + _dot((u * u).astype(BF16), w2_ref[c * ff_chunk:(c + 1) * ff_chunk, :])
    if final_norm:
        acc = _rmsnorm(acc, fg_ref[...])
    o_ref[...] = acc


def _out_ffn(h2d, x2d, w_out_all, mixer_layer, norm_g, w1_all, w2_all, layer, final_g,
             *, final_norm, tm, ff_chunk):
    t, d = x2d.shape
    v_w, d_ff = h2d.shape[1], w1_all.shape[2]
    const = lambda i: (0, 0)
    resident = pl.Buffered(1)
    return pl.pallas_call(
        functools.partial(_out_ffn_kernel, ff_chunk=ff_chunk, final_norm=final_norm),
        grid=(t // tm,),
        in_specs=[
            pl.BlockSpec((tm, v_w), lambda i: (i, 0)),
            pl.BlockSpec((tm, d), lambda i: (i, 0)),
            pl.BlockSpec((None, v_w, d), lambda i: (mixer_layer, 0, 0), pipeline_mode=resident),
            pl.BlockSpec((1, d), const),
            pl.BlockSpec((None, d, d_ff), lambda i: (layer, 0, 0), pipeline_mode=resident),
            pl.BlockSpec((None, d_ff, d), lambda i: (layer, 0, 0), pipeline_mode=resident),
            pl.BlockSpec((1, d), const),
        ],
        out_specs=pl.BlockSpec((tm, d), lambda i: (i, 0)),
        out_shape=jax.ShapeDtypeStruct((t, d), F32),
        compiler_params=pltpu.CompilerParams(
            dimension_semantics=("parallel",), vmem_limit_bytes=VMEM_LIMIT_BYTES),
        name="out_proj_ffn",
    )(h2d, x2d, w_out_all, norm_g, w1_all, w2_all, final_g)


PROJ_ROWS = 1024
FFN_ROWS = 1024
FFN_CHUNK = 1024
MLSTM_CHUNK = 256
GLA_CHUNK = 128
MIXER_ROWS = 4
GLA_DIRECT = 8


def _pad_lanes(a):
    return jnp.pad(a, ((0, 0), (0, LANE - a.shape[1])))


def kernel(x, norm_mix_g, norm_ffn_g, final_norm_g, mlstm_w_in, mlstm_b_gate, mlstm_out_norm_g, mlstm_w_out,
           gla_w_in, gla_w_gate_up, gla_b_gate, gla_out_norm_g, gla_w_out, ffn_w1, ffn_w2):
    b, s, d = x.shape
    depth = norm_mix_g.shape[0]
    heads = N_HEADS
    qk_w, v_w = d // 2, d
    n_main = 2 * qk_w + 2 * v_w
    dk = qk_w // heads
    t = b * s
    x2d = x.reshape(t, d)
    final_g = final_norm_g.reshape(1, d)
    n_cols = n_main // qk_w

    mlstm_w_in_b, gla_w_in_b = mlstm_w_in.astype(BF16), gla_w_in.astype(BF16)
    mlstm_w_out_b, gla_w_out_b = mlstm_w_out.astype(BF16), gla_w_out.astype(BF16)
    ffn_w1_b, ffn_w2_b = ffn_w1.astype(BF16), ffn_w2.astype(BF16)

    for i in range(depth):
        j = i // 2
        norm_g = norm_mix_g[i].reshape(1, d)
        if i % 2 == 0:
            scales = (1.0, dk ** -0.5) + (1.0,) * (n_cols - 2)
            main, gates = _input_projection(
                x2d, norm_g, mlstm_w_in_b, j, n_main, _pad_lanes(mlstm_w_in[j][:, n_main:]).astype(BF16),
                _pad_lanes(mlstm_b_gate[j].reshape(1, -1)), scales, qk_w, PROJ_ROWS)
            mixed = _mlstm_mixer(main.reshape(b, s, n_main), gates.reshape(b, s, LANE),
                                 mlstm_out_norm_g[j].reshape(1, v_w),
                                 heads=heads, qk_w=qk_w, v_w=v_w, chunk=MLSTM_CHUNK, rows=MIXER_ROWS)
            w_out_b = mlstm_w_out_b
        else:
            scales = (dk ** -0.5,) + (1.0,) * (n_cols - 1)
            main, z_low = _input_projection(
                x2d, norm_g, gla_w_in_b, j, n_main, _pad_lanes(gla_w_in[j][:, n_main:]).astype(BF16),
                jnp.zeros((1, LANE), F32), scales, qk_w, PROJ_ROWS)
            rank = gla_w_gate_up.shape[1]
            w_up = jnp.pad(gla_w_gate_up[j], ((0, LANE - rank), (0, 0))).astype(BF16)
            mixed = _gla_mixer(main.reshape(b, s, n_main), z_low.reshape(b, s, LANE), w_up,
                               gla_b_gate[j].reshape(1, qk_w), gla_out_norm_g[j].reshape(1, v_w),
                               heads=heads, qk_w=qk_w, v_w=v_w, chunk=GLA_CHUNK, direct=GLA_DIRECT,
                               rows=MIXER_ROWS)
            w_out_b = gla_w_out_b
        x2d = _out_ffn(mixed.reshape(t, v_w), x2d, w_out_b, j, norm_ffn_g[i].reshape(1, d),
                       ffn_w1_b, ffn_w2_b, i, final_g,
                       final_norm=(i == depth - 1), tm=FFN_ROWS, ff_chunk=FFN_CHUNK)
    return x2d.reshape(b, s, d)
```

```python
import functools

import math

import numpy as np

import jax
import jax.numpy as jnp
from jax import lax
from jax.experimental import pallas as pl
from jax.experimental.pallas import tpu as pltpu

EPS = 1e-6
N_HEADS = 4
GLA_TAU = 16.0
LOG2_E = math.log2(math.e)

LANE = 128
SUBLANE = 8
VMEM_LIMIT_BYTES = 56 * 1024 * 1024

F32 = jnp.float32
BF16 = jnp.bfloat16

_NT = (((1,), (1,)), ((), ()))
_TN = (((0,), (0,)), ((), ()))


def _dot(a, b):
    return jnp.dot(a, b, preferred_element_type=F32)


def _dot_nt(a, b):
    return lax.dot_general(a, b, _NT, preferred_element_type=F32)


def _dot_tn(a, b):
    return lax.dot_general(a, b, _TN, preferred_element_type=F32)


def _rmsnorm(x, g):
    return x * lax.rsqrt(jnp.mean(x * x, axis=-1, keepdims=True) + EPS) * g


def _log_sigmoid(x):
    return jnp.minimum(x, 0.0) - jnp.log(1.0 + jnp.exp(-jnp.abs(x)))


def _cumsum_matrix(chunk):
    tril = np.tril(np.ones((chunk, chunk), np.float32))
    return jnp.asarray(np.concatenate([tril] * 3, axis=1), dtype=BF16)


def _cumsum_rows(tril3, x):
    hi = x.astype(BF16)
    r1 = x - hi.astype(F32)
    mid = r1.astype(BF16)
    lo = (r1 - mid.astype(F32)).astype(BF16)
    return _dot(tril3, jnp.concatenate([hi, mid, lo], axis=0))


def _proj_kernel(x_ref, g_ref, w_ref, wg_ref, bg_ref, main_ref, gate_ref, wb_ref, *, col_scales, tn):
    @pl.when(pl.program_id(0) == 0)
    def _():
        for c in range(len(col_scales)):
            wb_ref[:, c * tn:(c + 1) * tn] = w_ref[:, c * tn:(c + 1) * tn].astype(BF16)

    xb = _rmsnorm(x_ref[...], g_ref[...]).astype(BF16)
    for c, scale in enumerate(col_scales):
        acc = _dot(xb, wb_ref[:, c * tn:(c + 1) * tn])
        if scale != 1.0:
            acc = acc * scale
        main_ref[:, c * tn:(c + 1) * tn] = acc.astype(BF16)
    gate_ref[...] = _dot(xb, wg_ref[...]) + bg_ref[...]


def _input_projection(x2d, norm_g, w_in_all, layer, n_main, w_gate, b_gate, col_scales, tn, tm):
    t, d = x2d.shape
    const = lambda i: (0, 0)
    return pl.pallas_call(
        functools.partial(_proj_kernel, col_scales=col_scales, tn=tn),
        grid=(t // tm,),
        in_specs=[
            pl.BlockSpec((tm, d), lambda i: (i, 0)),
            pl.BlockSpec((1, d), const),
            pl.BlockSpec((None, d, n_main), lambda i: (layer, 0, 0), pipeline_mode=pl.Buffered(1)),
            pl.BlockSpec((d, LANE), const),
            pl.BlockSpec((1, LANE), const),
        ],
        out_specs=[
            pl.BlockSpec((tm, n_main), lambda i: (i, 0)),
            pl.BlockSpec((tm, LANE), lambda i: (i, 0)),
        ],
        out_shape=[
            jax.ShapeDtypeStruct((t, n_main), BF16),
            jax.ShapeDtypeStruct((t, LANE), F32),
        ],
        scratch_shapes=[pltpu.VMEM((d, n_main), BF16)],
        compiler_params=pltpu.CompilerParams(
            dimension_semantics=("arbitrary",), vmem_limit_bytes=VMEM_LIMIT_BYTES),
        name="input_projection",
    )(x2d, norm_g, w_in_all, w_gate, b_gate)


def _cummax_rows(x):
    L = x.shape[0]
    tiles = L // SUBLANE
    x3 = x.reshape(tiles, SUBLANE, LANE)
    sub = lax.broadcasted_iota(jnp.int32, x3.shape, 1)
    shift = 1
    while shift < SUBLANE:
        x3 = jnp.maximum(x3, jnp.where(sub >= shift, pltpu.roll(x3, shift, 1), -jnp.inf))
        shift *= 2
    tot = jnp.broadcast_to(x3[:, SUBLANE - 1:SUBLANE, :], x3.shape)
    before = jnp.concatenate([jnp.full((1, SUBLANE, LANE), -jnp.inf, F32), tot[:tiles - 1]], axis=0)
    shift = 1
    while shift < tiles:
        pad = jnp.full((shift, SUBLANE, LANE), -jnp.inf, F32)
        before = jnp.maximum(before, jnp.concatenate([pad, before[:tiles - shift]], axis=0))
        shift *= 2
    return jnp.maximum(x3, before).reshape(L, LANE)


def _mlstm_kernel(q_ref, k_ref, v_ref, o_ref, gt_ref, gain_ref, tril_ref, out_ref, c_ref, m_ref,
                  *, heads, dk, dv, chunk):
    L = chunk

    @pl.when(pl.program_id(1) == 0)
    def _():
        c_ref[...] = jnp.zeros_like(c_ref)
        m_ref[...] = jnp.zeros_like(m_ref)

    causal = lax.broadcasted_iota(jnp.int32, (L, L), 1) <= lax.broadcasted_iota(jnp.int32, (L, L), 0)
    lane = lax.broadcasted_iota(jnp.int32, (L, LANE), 1)
    head_lanes = lane < heads
    ones_col = [jnp.where(lane == h, 1.0, 0.0).astype(BF16) for h in range(heads)]

    pending = None
    for bi in range(q_ref.shape[0]):
        gates = gt_ref[bi]
        pre = jnp.where(head_lanes, gates, _log_sigmoid(gates)) * LOG2_E
        cum = _cumsum_rows(tril_ref[...], pre)
        rows_t = jnp.where(head_lanes, pre, cum).T
        li = jnp.where(head_lanes, pre, 0.0)
        b = jnp.where(head_lanes, pltpu.roll(cum, LANE - heads, 1), 0.0)
        m_prev = m_ref[bi]
        g_tot = b[L - 1:L, :]

        mm = jnp.maximum(m_prev, _cummax_rows(li - b))
        w_inter = jnp.exp2(m_prev - mm)

        a = g_tot - b + li
        m_new = jnp.maximum(g_tot + m_prev, jnp.max(a, axis=0, keepdims=True))
        w_s = jnp.exp2(a - m_new)
        decay = jnp.exp2(g_tot + m_prev - m_new)
        m_ref[bi] = m_new

        nums, den = [], jnp.zeros((L, LANE), F32)
        for h in range(heads):
            q = q_ref[bi, :, h * dk:(h + 1) * dk]
            k = k_ref[bi, :, h * dk:(h + 1) * dk]
            v_ext = jnp.concatenate([v_ref[bi, :, h * dv:(h + 1) * dv], ones_col[h]], axis=1)
            c_row = rows_t[h:h + 1, :] - rows_t[heads + h:heads + h + 1, :]
            p = jnp.where(causal, jnp.exp2(c_row - mm[:, h:h + 1]), 0.0)
            s = (_dot_nt(q, k) * p).astype(BF16)
            q_w = (q.astype(F32) * w_inter[:, h:h + 1]).astype(BF16)
            state = c_ref[bi, h]
            num = _dot(q_w, state.astype(BF16)) + _dot(s, v_ext)
            nums.append(num[:, :dv])
            den = den + num[:, dv:]

            kw = (k.astype(F32) * w_s[:, h:h + 1]).astype(BF16)
            c_ref[bi, h] = decay[:, h:h + 1] * state + _dot_tn(kw, v_ext)

            if pending is not None:
                _mlstm_epilogue(o_ref, gain_ref, out_ref, *pending, h, dv)

        inv = 1.0 / jnp.maximum(jnp.abs(den), jnp.exp2(-(b + mm)))
        pending = (bi, nums, inv)
    for h in range(heads):
        _mlstm_epilogue(o_ref, gain_ref, out_ref, *pending, h, dv)


def _mlstm_epilogue(o_ref, gain_ref, out_ref, bi, nums, inv, h, dv):
    cols = slice(h * dv, (h + 1) * dv)
    gated = jax.nn.sigmoid(o_ref[bi, :, cols].astype(F32)) * (nums[h] * inv[:, h:h + 1])
    out_ref[bi, :, cols] = _rmsnorm(gated, gain_ref[:, cols]).astype(BF16)


def _mlstm_mixer(main, gates, out_gain, *, heads, qk_w, v_w, chunk, rows):
    b, s, _ = main.shape
    assert 2 * qk_w == v_w and s % chunk == 0 and b % rows == 0 and 2 * heads <= LANE
    dk, dv = qk_w // heads, v_w // heads
    tril3 = _cumsum_matrix(chunk)
    const = lambda i, j: (0, 0)
    return pl.pallas_call(
        functools.partial(_mlstm_kernel, heads=heads, dk=dk, dv=dv, chunk=chunk),
        grid=(b // rows, s // chunk),
        in_specs=[
            pl.BlockSpec((rows, chunk, qk_w), lambda i, j: (i, j, 0)),
            pl.BlockSpec((rows, chunk, qk_w), lambda i, j: (i, j, 1)),
            pl.BlockSpec((rows, chunk, v_w), lambda i, j: (i, j, 1)),
            pl.BlockSpec((rows, chunk, v_w), lambda i, j: (i, j, 2)),
            pl.BlockSpec((rows, chunk, LANE), lambda i, j: (i, j, 0)),
            pl.BlockSpec((1, v_w), const),
            pl.BlockSpec(tril3.shape, const),
        ],
        out_specs=pl.BlockSpec((rows, chunk, v_w), lambda i, j: (i, j, 0)),
        out_shape=jax.ShapeDtypeStruct((b, s, v_w), BF16),
        scratch_shapes=[
            pltpu.VMEM((rows, heads, dk, dv + LANE), F32),
            pltpu.VMEM((rows, 1, LANE), F32),
        ],
        compiler_params=pltpu.CompilerParams(
            dimension_semantics=("parallel", "arbitrary"), vmem_limit_bytes=VMEM_LIMIT_BYTES),
        name="mlstm_mixer",
    )(main, main, main, main, gates, out_gain, tril3)


def _gla_levels(chunk, direct):
    levels, w = [], direct
    while w < chunk:
        levels.append(w)
        w *= 2
    return tuple(levels)


def _gla_segment_matrix(chunk, direct):
    j = np.arange(chunk)[:, None]
    s = np.arange(chunk)[None, :]
    blocks = [s <= j]
    for w in _gla_levels(chunk, direct):
        mid = (j // (2 * w)) * (2 * w) + w - 1
        upper = (j % (2 * w)) >= w
        blocks.append(np.where(upper, (s > mid) & (s <= j), (s > j) & (s <= mid)))
    seg = np.concatenate(blocks, axis=0).astype(np.float32)
    return jnp.asarray(np.concatenate([seg, seg], axis=1), dtype=BF16)


def _gla_kernel(q_ref, k_ref, v_ref, r_ref, zl_ref, wup_ref, bup_ref, gain_ref, seg_ref, out_ref, s_ref,
                *, heads, dk, dv, chunk, direct):
    L = chunk
    levels = _gla_levels(L, direct)

    @pl.when(pl.program_id(1) == 0)
    def _():
        s_ref[...] = jnp.zeros_like(s_ref)

    row = lax.broadcasted_iota(jnp.int32, (L, L), 0)
    col = lax.broadcasted_iota(jnp.int32, (L, L), 1)
    diff = row ^ col
    level_idx = sum(jnp.where(diff >= w, 1, 0) for w in levels)
    code = jnp.where(col > row, -1, jnp.where(diff < direct, row - col, direct - 1 + level_idx))
    pair_masks = [code == c for c in range(direct + len(levels))]
    row_d = lax.broadcasted_iota(jnp.int32, (L, dk), 0)
    upper_masks = [(row_d & w) != 0 for w in levels]

    def decays(bi):
        z = _dot(zl_ref[bi].astype(BF16), wup_ref[...]) + bup_ref[...]
        la = _log_sigmoid(z) * (LOG2_E / GLA_TAU)
        la_hi = la.astype(BF16)
        la_lo = (la - la_hi.astype(F32)).astype(BF16)
        seg = _dot(seg_ref[...], jnp.concatenate([la_hi, la_lo], axis=0))
        return jnp.exp2(la), seg

    n_rows = q_ref.shape[0]
    pending = None
    next_decays = decays(0)
    for bi in range(n_rows):
        step, seg = next_decays
        if bi + 1 < n_rows:
            next_decays = decays(bi + 1)

        for h in range(heads):
            lanes = slice(h * dk, (h + 1) * dk)
            b = seg[0:L, lanes]
            g_tot = b[L - 1:L, :]
            qf = q_ref[bi, :, lanes].astype(F32)
            kf = k_ref[bi, :, lanes].astype(F32)
            v = v_ref[bi, :, h * dv:(h + 1) * dv]
            s_prev = s_ref[bi, h]

            o = _dot((qf * jnp.exp2(b)).astype(BF16), s_prev.astype(BF16))

            a_mat = jnp.zeros((L, L), F32)
            q3 = qf.reshape(L // SUBLANE, SUBLANE, dk)
            step3 = step[:, lanes].reshape(L // SUBLANE, SUBLANE, dk)
            y = kf.reshape(L // SUBLANE, SUBLANE, dk)
            for t in range(direct):
                if t > 0:
                    y = step3 * pltpu.roll(y, 1, 1)
                r = jnp.sum(q3 * y, axis=-1, keepdims=True).reshape(L, 1)
                a_mat = jnp.where(pair_masks[t], r, a_mat)

            for i, w in enumerate(levels):
                decay = jnp.exp2(seg[(1 + i) * L:(2 + i) * L, lanes])
                mixed = (jnp.where(upper_masks[i], qf, kf) * decay).astype(BF16)
                a_mat = jnp.where(pair_masks[direct + i], _dot_nt(mixed, mixed), a_mat)

            o = o + _dot(a_mat.astype(BF16), v)

            kg = (kf * jnp.exp2(g_tot - b)).astype(BF16)
            g_col = jnp.broadcast_to(jnp.exp2(g_tot), (LANE, dk)).T
            g_col = jnp.concatenate([g_col] * (dv // LANE), axis=1)
            s_ref[bi, h] = g_col * s_prev + _dot_tn(kg, v)

            if pending is not None:
                _gla_epilogue(r_ref, gain_ref, out_ref, *pending, dv)
            pending = (bi, h, o)
    _gla_epilogue(r_ref, gain_ref, out_ref, *pending, dv)


def _gla_epilogue(r_ref, gain_ref, out_ref, bi, h, o, dv):
    cols = slice(h * dv, (h + 1) * dv)
    rr = r_ref[bi, :, cols].astype(F32)
    out_ref[bi, :, cols] = (rr * jax.nn.sigmoid(rr) * _rmsnorm(o, gain_ref[:, cols])).astype(BF16)


def _gla_mixer(main, z_low, w_up, b_up, out_gain, *, heads, qk_w, v_w, chunk, direct, rows):
    b, s, _ = main.shape
    assert 2 * qk_w == v_w and s % chunk == 0 and SUBLANE % direct == 0 and b % rows == 0
    dk, dv = qk_w // heads, v_w // heads
    seg_matrix = _gla_segment_matrix(chunk, direct)
    const = lambda i, j: (0, 0)
    return pl.pallas_call(
        functools.partial(_gla_kernel, heads=heads, dk=dk, dv=dv, chunk=chunk, direct=direct),
        grid=(b // rows, s // chunk),
        in_specs=[
            pl.BlockSpec((rows, chunk, qk_w), lambda i, j: (i, j, 0)),
            pl.BlockSpec((rows, chunk, qk_w), lambda i, j: (i, j, 1)),
            pl.BlockSpec((rows, chunk, v_w), lambda i, j: (i, j, 1)),
            pl.BlockSpec((rows, chunk, v_w), lambda i, j: (i, j, 2)),
            pl.BlockSpec((rows, chunk, LANE), lambda i, j: (i, j, 0)),
            pl.BlockSpec((LANE, qk_w), const),
            pl.BlockSpec((1, qk_w), const),
            pl.BlockSpec((1, v_w), const),
            pl.BlockSpec(seg_matrix.shape, const),
        ],
        out_specs=pl.BlockSpec((rows, chunk, v_w), lambda i, j: (i, j, 0)),
        out_shape=jax.ShapeDtypeStruct((b, s, v_w), BF16),
        scratch_shapes=[pltpu.VMEM((rows, heads, dk, dv), F32)],
        compiler_params=pltpu.CompilerParams(
            dimension_semantics=("parallel", "arbitrary"), vmem_limit_bytes=VMEM_LIMIT_BYTES),
        name="gla_mixer",
    )(main, main, main, main, z_low, w_up, b_up, out_gain, seg_matrix)


def _out_ffn_kernel(h_ref, x_ref, wo_ref, g_ref, w1_ref, w2_ref, fg_ref, o_ref, *, ff_chunk, final_norm):
    x1 = x_ref[...] + _dot(h_ref[...], wo_ref[...])
    hn = _rmsnorm(x1, g_ref[...]).astype(BF16)
    acc = x1
    for c in range(w1_ref.shape[1] // ff_chunk):
        u = jnp.maximum(_dot(hn, w1_ref[:, c * ff_chunk:(c + 1) * ff_chunk]), 0.0)
        acc = acc + _dot((u * u).astype(BF16), w2_ref[c * ff_chunk:(c + 1) * ff_chunk, :])
    if final_norm:
        acc = _rmsnorm(acc, fg_ref[...])
    o_ref[...] = acc


def _out_ffn(h2d, x2d, w_out_all, mixer_layer, norm_g, w1_all, w2_all, layer, final_g,
             *, final_norm, tm, ff_chunk):
    t, d = x2d.shape
    v_w, d_ff = h2d.shape[1], w1_all.shape[2]
    const = lambda i: (0, 0)
    resident = pl.Buffered(1)
    return pl.pallas_call(
        functools.partial(_out_ffn_kernel, ff_chunk=ff_chunk, final_norm=final_norm),
        grid=(t // tm,),
        in_specs=[
            pl.BlockSpec((tm, v_w), lambda i: (i, 0)),
            pl.BlockSpec((tm, d), lambda i: (i, 0)),
            pl.BlockSpec((None, v_w, d), lambda i: (mixer_layer, 0, 0), pipeline_mode=resident),
            pl.BlockSpec((1, d), const),
            pl.BlockSpec((None, d, d_ff), lambda i: (layer, 0, 0), pipeline_mode=resident),
            pl.BlockSpec((None, d_ff, d), lambda i: (layer, 0, 0), pipeline_mode=resident),
            pl.BlockSpec((1, d), const),
        ],
        out_specs=pl.BlockSpec((tm, d), lambda i: (i, 0)),
        out_shape=jax.ShapeDtypeStruct((t, d), F32),
        compiler_params=pltpu.CompilerParams(
            dimension_semantics=("parallel",), vmem_limit_bytes=VMEM_LIMIT_BYTES),
        name="out_proj_ffn",
    )(h2d, x2d, w_out_all, norm_g, w1_all, w2_all, final_g)


PROJ_ROWS = 1024
FFN_ROWS = 1024
FFN_CHUNK = 1024
MLSTM_CHUNK = 256
GLA_CHUNK = 128
MIXER_ROWS = 4
GLA_DIRECT = 8


def _pad_lanes(a):
    return jnp.pad(a, ((0, 0), (0, LANE - a.shape[1])))


def kernel(x, norm_mix_g, norm_ffn_g, final_norm_g, mlstm_w_in, mlstm_b_gate, mlstm_out_norm_g, mlstm_w_out,
           gla_w_in, gla_w_gate_up, gla_b_gate, gla_out_norm_g, gla_w_out, ffn_w1, ffn_w2):
    b, s, d = x.shape
    depth = norm_mix_g.shape[0]
    heads = N_HEADS
    qk_w, v_w = d // 2, d
    n_main = 2 * qk_w + 2 * v_w
    dk = qk_w // heads
    t = b * s
    x2d = x.reshape(t, d)
    final_g = final_norm_g.reshape(1, d)
    n_cols = n_main // qk_w

    mlstm_w_out_b, gla_w_out_b = mlstm_w_out.astype(BF16), gla_w_out.astype(BF16)
    ffn_w1_b, ffn_w2_b = ffn_w1.astype(BF16), ffn_w2.astype(BF16)

    for i in range(depth):
        j = i // 2
        norm_g = norm_mix_g[i].reshape(1, d)
        if i % 2 == 0:
            scales = (1.0, dk ** -0.5) + (1.0,) * (n_cols - 2)
            main, gates = _input_projection(
                x2d, norm_g, mlstm_w_in, j, n_main, _pad_lanes(mlstm_w_in[j][:, n_main:]).astype(BF16),
                _pad_lanes(mlstm_b_gate[j].reshape(1, -1)), scales, qk_w, PROJ_ROWS)
            mixed = _mlstm_mixer(main.reshape(b, s, n_main), gates.reshape(b, s, LANE),
                                 mlstm_out_norm_g[j].reshape(1, v_w),
                                 heads=heads, qk_w=qk_w, v_w=v_w, chunk=MLSTM_CHUNK, rows=MIXER_ROWS)
            w_out_b = mlstm_w_out_b
        else:
            scales = (dk ** -0.5,) + (1.0,) * (n_cols - 1)
            main, z_low = _input_projection(
                x2d, norm_g, gla_w_in, j, n_main, _pad_lanes(gla_w_in[j][:, n_main:]).astype(BF16),
                jnp.zeros((1, LANE), F32), scales, qk_w, PROJ_ROWS)
            rank = gla_w_gate_up.shape[1]
            w_up = jnp.pad(gla_w_gate_up[j], ((0, LANE - rank), (0, 0))).astype(BF16)
            mixed = _gla_mixer(main.reshape(b, s, n_main), z_low.reshape(b, s, LANE), w_up,
                               gla_b_gate[j].reshape(1, qk_w), gla_out_norm_g[j].reshape(1, v_w),
                               heads=heads, qk_w=qk_w, v_w=v_w, chunk=GLA_CHUNK, direct=GLA_DIRECT,
                               rows=MIXER_ROWS)
            w_out_b = gla_w_out_b
        x2d = _out_ffn(mixed.reshape(t, v_w), x2d, w_out_b, j, norm_ffn_g[i].reshape(1, d),
                       ffn_w1_b, ffn_w2_b, i, final_g,
                       final_norm=(i == depth - 1), tm=FFN_ROWS, ff_chunk=FFN_CHUNK)
    return x2d.reshape(b, s, d)
```

```python
import functools

import math

import numpy as np

import jax
import jax.numpy as jnp
from jax import lax
from jax.experimental import pallas as pl
from jax.experimental.pallas import tpu as pltpu

EPS = 1e-6
N_HEADS = 4
GLA_TAU = 16.0
LOG2_E = math.log2(math.e)

LANE = 128
SUBLANE = 8
VMEM_LIMIT_BYTES = 56 * 1024 * 1024

F32 = jnp.float32
BF16 = jnp.bfloat16

_NT = (((1,), (1,)), ((), ()))
_TN = (((0,), (0,)), ((), ()))


def _dot(a, b):
    return jnp.dot(a, b, preferred_element_type=F32)


def _dot_nt(a, b):
    return lax.dot_general(a, b, _NT, preferred_element_type=F32)


def _dot_tn(a, b):
    return lax.dot_general(a, b, _TN, preferred_element_type=F32)


def _rmsnorm(x, g):
    return x * lax.rsqrt(jnp.mean(x * x, axis=-1, keepdims=True) + EPS) * g


def _log_sigmoid(x):
    return jnp.minimum(x, 0.0) - jnp.log(1.0 + jnp.exp(-jnp.abs(x)))


def _cumsum_matrix(chunk):
    tril = np.tril(np.ones((chunk, chunk), np.float32))
    return jnp.asarray(np.concatenate([tril] * 3, axis=1), dtype=BF16)


def _cumsum_rows(tril3, x):
    hi = x.astype(BF16)
    r1 = x - hi.astype(F32)
    mid = r1.astype(BF16)
    lo = (r1 - mid.astype(F32)).astype(BF16)
    return _dot(tril3, jnp.concatenate([hi, mid, lo], axis=0))


def _proj_kernel(x_ref, g_ref, w_ref, wg_ref, bg_ref, main_ref, gate_ref, *, col_scales, tn):
    xb = _rmsnorm(x_ref[...], g_ref[...]).astype(BF16)
    for c, scale in enumerate(col_scales):
        acc = _dot(xb, w_ref[:, c * tn:(c + 1) * tn])
        if scale != 1.0:
            acc = acc * scale
        main_ref[:, c * tn:(c + 1) * tn] = acc.astype(BF16)
    gate_ref[...] = _dot(xb, wg_ref[...]) + bg_ref[...]


def _input_projection(x2d, norm_g, w_in_all, layer, n_main, w_gate, b_gate, col_scales, tn, tm):
    t, d = x2d.shape
    const = lambda i: (0, 0)
    return pl.pallas_call(
        functools.partial(_proj_kernel, col_scales=col_scales, tn=tn),
        grid=(t // tm,),
        in_specs=[
            pl.BlockSpec((tm, d), lambda i: (i, 0)),
            pl.BlockSpec((1, d), const),
            pl.BlockSpec((None, d, n_main), lambda i: (layer, 0, 0)),
            pl.BlockSpec((d, LANE), const),
            pl.BlockSpec((1, LANE), const),
        ],
        out_specs=[
            pl.BlockSpec((tm, n_main), lambda i: (i, 0)),
            pl.BlockSpec((tm, LANE), lambda i: (i, 0)),
        ],
        out_shape=[
            jax.ShapeDtypeStruct((t, n_main), BF16),
            jax.ShapeDtypeStruct((t, LANE), F32),
        ],
        compiler_params=pltpu.CompilerParams(
            dimension_semantics=("parallel",), vmem_limit_bytes=VMEM_LIMIT_BYTES),
        name="input_projection",
    )(x2d, norm_g, w_in_all, w_gate, b_gate)


def _cummax_rows(x):
    L = x.shape[0]
    tiles = L // SUBLANE
    x3 = x.reshape(tiles, SUBLANE, LANE)
    sub = lax.broadcasted_iota(jnp.int32, x3.shape, 1)
    shift = 1
    while shift < SUBLANE:
        x3 = jnp.maximum(x3, jnp.where(sub >= shift, pltpu.roll(x3, shift, 1), -jnp.inf))
        shift *= 2
    tot = jnp.broadcast_to(x3[:, SUBLANE - 1:SUBLANE, :], x3.shape)
    before = jnp.concatenate([jnp.full((1, SUBLANE, LANE), -jnp.inf, F32), tot[:tiles - 1]], axis=0)
    shift = 1
    while shift < tiles:
        pad = jnp.full((shift, SUBLANE, LANE), -jnp.inf, F32)
        before = jnp.maximum(before, jnp.concatenate([pad, before[:tiles - shift]], axis=0))
        shift *= 2
    return jnp.maximum(x3, before).reshape(L, LANE)


def _mlstm_kernel(q_ref, k_ref, v_ref, o_ref, gt_ref, gain_ref, tril_ref, out_ref, c_ref, m_ref,
                  *, heads, dk, dv, chunk):
    L = chunk

    @pl.when(pl.program_id(1) == 0)
    def _():
        c_ref[...] = jnp.zeros_like(c_ref)
        m_ref[...] = jnp.zeros_like(m_ref)

    causal = lax.broadcasted_iota(jnp.int32, (L, L), 1) <= lax.broadcasted_iota(jnp.int32, (L, L), 0)
    lane = lax.broadcasted_iota(jnp.int32, (L, LANE), 1)
    head_lanes = lane < heads
    ones_col = [jnp.where(lane == h, 1.0, 0.0).astype(BF16) for h in range(heads)]

    def gate_terms(bi):
        gates = gt_ref[bi]
        pre = jnp.where(head_lanes, gates, _log_sigmoid(gates)) * LOG2_E
        cum = _cumsum_rows(tril_ref[...], pre)
        rows_t = jnp.where(head_lanes, pre, cum).T
        li = jnp.where(head_lanes, pre, 0.0)
        b = jnp.where(head_lanes, pltpu.roll(cum, LANE - heads, 1), 0.0)
        m_prev = m_ref[bi]
        g_tot = b[L - 1:L, :]

        mm = jnp.maximum(m_prev, _cummax_rows(li - b))
        w_inter = jnp.exp2(m_prev - mm)

        a = g_tot - b + li
        m_new = jnp.maximum(g_tot + m_prev, jnp.max(a, axis=0, keepdims=True))
        w_s = jnp.exp2(a - m_new)
        decay = jnp.exp2(g_tot + m_prev - m_new)
        m_ref[bi] = m_new
        return rows_t, b, mm, w_inter, w_s, decay

    n_rows = q_ref.shape[0]
    pending = None
    next_terms = gate_terms(0)
    for bi in range(n_rows):
        rows_t, b, mm, w_inter, w_s, decay = next_terms
        if bi + 1 < n_rows:
            next_terms = gate_terms(bi + 1)

        nums, den = [], jnp.zeros((L, LANE), F32)
        for h in range(heads):
            q = q_ref[bi, :, h * dk:(h + 1) * dk]
            k = k_ref[bi, :, h * dk:(h + 1) * dk]
            v_ext = jnp.concatenate([v_ref[bi, :, h * dv:(h + 1) * dv], ones_col[h]], axis=1)
            c_row = rows_t[h:h + 1, :] - rows_t[heads + h:heads + h + 1, :]
            p = jnp.where(causal, jnp.exp2(c_row - mm[:, h:h + 1]), 0.0)
            s = (_dot_nt(q, k) * p).astype(BF16)
            q_w = (q.astype(F32) * w_inter[:, h:h + 1]).astype(BF16)
            state = c_ref[bi, h]
            num = _dot(q_w, state.astype(BF16)) + _dot(s, v_ext)
            nums.append(num[:, :dv])
            den = den + num[:, dv:]

            kw = (k.astype(F32) * w_s[:, h:h + 1]).astype(BF16)
            c_ref[bi, h] = decay[:, h:h + 1] * state + _dot_tn(kw, v_ext)

            if pending is not None:
                _mlstm_epilogue(o_ref, gain_ref, out_ref, *pending, h, dv)

        inv = 1.0 / jnp.maximum(jnp.abs(den), jnp.exp2(-(b + mm)))
        pending = (bi, nums, inv)
    for h in range(heads):
        _mlstm_epilogue(o_ref, gain_ref, out_ref, *pending, h, dv)


def _mlstm_epilogue(o_ref, gain_ref, out_ref, bi, nums, inv, h, dv):
    cols = slice(h * dv, (h + 1) * dv)
    gated = jax.nn.sigmoid(o_ref[bi, :, cols].astype(F32)) * (nums[h] * inv[:, h:h + 1])
    out_ref[bi, :, cols] = _rmsnorm(gated, gain_ref[:, cols]).astype(BF16)


def _mlstm_mixer(main, gates, out_gain, *, heads, qk_w, v_w, chunk, rows):
    b, s, _ = main.shape
    assert 2 * qk_w == v_w and s % chunk == 0 and b % rows == 0 and 2 * heads <= LANE
    dk, dv = qk_w // heads, v_w // heads
    tril3 = _cumsum_matrix(chunk)
    const = lambda i, j: (0, 0)
    return pl.pallas_call(
        functools.partial(_mlstm_kernel, heads=heads, dk=dk, dv=dv, chunk=chunk),
        grid=(b // rows, s // chunk),
        in_specs=[
            pl.BlockSpec((rows, chunk, qk_w), lambda i, j: (i, j, 0)),
            pl.BlockSpec((rows, chunk, qk_w), lambda i, j: (i, j, 1)),
            pl.BlockSpec((rows, chunk, v_w), lambda i, j: (i, j, 1)),
            pl.BlockSpec((rows, chunk, v_w), lambda i, j: (i, j, 2)),
            pl.BlockSpec((rows, chunk, LANE), lambda i, j: (i, j, 0)),
            pl.BlockSpec((1, v_w), const),
            pl.BlockSpec(tril3.shape, const),
        ],
        out_specs=pl.BlockSpec((rows, chunk, v_w), lambda i, j: (i, j, 0)),
        out_shape=jax.ShapeDtypeStruct((b, s, v_w), BF16),
        scratch_shapes=[
            pltpu.VMEM((rows, heads, dk, dv + LANE), F32),
            pltpu.VMEM((rows, 1, LANE), F32),
        ],
        compiler_params=pltpu.CompilerParams(
            dimension_semantics=("parallel", "arbitrary"), vmem_limit_bytes=VMEM_LIMIT_BYTES),
        name="mlstm_mixer",
    )(main, main, main, main, gates, out_gain, tril3)


def _gla_levels(chunk, direct):
    levels, w = [], direct
    while w < chunk:
        levels.append(w)
        w *= 2
    return tuple(levels)


def _gla_segment_matrix(chunk, direct):
    j = np.arange(chunk)[:, None]
    s = np.arange(chunk)[None, :]
    blocks = [s <= j]
    for w in _gla_levels(chunk, direct):
        mid = (j // (2 * w)) * (2 * w) + w - 1
        upper = (j % (2 * w)) >= w
        blocks.append(np.where(upper, (s > mid) & (s <= j), (s > j) & (s <= mid)))
    seg = np.concatenate(blocks, axis=0).astype(np.float32)
    return jnp.asarray(np.concatenate([seg, seg], axis=1), dtype=BF16)


def _gla_kernel(q_ref, k_ref, v_ref, r_ref, zl_ref, wup_ref, bup_ref, gain_ref, seg_ref, out_ref, s_ref,
                *, heads, dk, dv, chunk, direct):
    L = chunk
    levels = _gla_levels(L, direct)

    @pl.when(pl.program_id(1) == 0)
    def _():
        s_ref[...] = jnp.zeros_like(s_ref)

    row = lax.broadcasted_iota(jnp.int32, (L, L), 0)
    col = lax.broadcasted_iota(jnp.int32, (L, L), 1)
    diff = row ^ col
    level_idx = sum(jnp.where(diff >= w, 1, 0) for w in levels)
    code = jnp.where(col > row, -1, jnp.where(diff < direct, row - col, direct - 1 + level_idx))
    pair_masks = [code == c for c in range(direct + len(levels))]
    row_d = lax.broadcasted_iota(jnp.int32, (L, dk), 0)
    upper_masks = [(row_d & w) != 0 for w in levels]

    def decays(bi):
        z = _dot(zl_ref[bi].astype(BF16), wup_ref[...]) + bup_ref[...]
        la = _log_sigmoid(z) * (LOG2_E / GLA_TAU)
        la_hi = la.astype(BF16)
        la_lo = (la - la_hi.astype(F32)).astype(BF16)
        seg = _dot(seg_ref[...], jnp.concatenate([la_hi, la_lo], axis=0))
        return jnp.exp2(la), seg

    def matrix_stage(bi, h, step, seg):
        lanes = slice(h * dk, (h + 1) * dk)
        b = seg[0:L, lanes]
        g_tot = b[L - 1:L, :]
        qf = q_ref[bi, :, lanes].astype(F32)
        kf = k_ref[bi, :, lanes].astype(F32)
        v = v_ref[bi, :, h * dv:(h + 1) * dv]
        s_prev = s_ref[bi, h]
        o_state = _dot((qf * jnp.exp2(b)).astype(BF16), s_prev.astype(BF16))

        level_products = []
        for i, w in enumerate(levels):
            decay = jnp.exp2(seg[(1 + i) * L:(2 + i) * L, lanes])
            mixed = (jnp.where(upper_masks[i], qf, kf) * decay).astype(BF16)
            level_products.append(_dot_nt(mixed, mixed))

        kg = (kf * jnp.exp2(g_tot - b)).astype(BF16)
        g_col = jnp.broadcast_to(jnp.exp2(g_tot), (LANE, dk)).T
        g_col = jnp.concatenate([g_col] * (dv // LANE), axis=1)
        s_ref[bi, h] = g_col * s_prev + _dot_tn(kg, v)
        return bi, h, qf, kf, step[:, lanes], o_state, level_products

    def vector_stage(bi, h, qf, kf, step_h, o_state, level_products):
        a_mat = jnp.zeros((L, L), F32)
        q3 = qf.reshape(L // SUBLANE, SUBLANE, dk)
        step3 = step_h.reshape(L // SUBLANE, SUBLANE, dk)
        y = kf.reshape(L // SUBLANE, SUBLANE, dk)
        for t in range(direct):
            if t > 0:
                y = step3 * pltpu.roll(y, 1, 1)
            r = jnp.sum(q3 * y, axis=-1, keepdims=True).reshape(L, 1)
            a_mat = jnp.where(pair_masks[t], r, a_mat)
        for i in range(len(levels)):
            a_mat = jnp.where(pair_masks[direct + i], level_products[i], a_mat)
        return bi, h, o_state + _dot(a_mat.astype(BF16), v_ref[bi, :, h * dv:(h + 1) * dv])

    n_rows = q_ref.shape[0]
    in_matrix = in_vector = None
    next_decays = decays(0)
    for bi in range(n_rows):
        step, seg = next_decays
        if bi + 1 < n_rows:
            next_decays = decays(bi + 1)
        for h in range(heads):
            new_matrix = matrix_stage(bi, h, step, seg)
            new_vector = vector_stage(*in_matrix) if in_matrix is not None else None
            if in_vector is not None:
                _gla_epilogue(r_ref, gain_ref, out_ref, *in_vector, dv)
            in_matrix, in_vector = new_matrix, new_vector
    last_vector = vector_stage(*in_matrix)
    if in_vector is not None:
        _gla_epilogue(r_ref, gain_ref, out_ref, *in_vector, dv)
    _gla_epilogue(r_ref, gain_ref, out_ref, *last_vector, dv)


def _gla_epilogue(r_ref, gain_ref, out_ref, bi, h, o, dv):
    cols = slice(h * dv, (h + 1) * dv)
    rr = r_ref[bi, :, cols].astype(F32)
    out_ref[bi, :, cols] = (rr * jax.nn.sigmoid(rr) * _rmsnorm(o, gain_ref[:, cols])).astype(BF16)


def _gla_mixer(main, z_low, w_up, b_up, out_gain, *, heads, qk_w, v_w, chunk, direct, rows):
    b, s, _ = main.shape
    assert 2 * qk_w == v_w and s % chunk == 0 and SUBLANE % direct == 0 and b % rows == 0
    dk, dv = qk_w // heads, v_w // heads
    seg_matrix = _gla_segment_matrix(chunk, direct)
    const = lambda i, j: (0, 0)
    return pl.pallas_call(
        functools.partial(_gla_kernel, heads=heads, dk=dk, dv=dv, chunk=chunk, direct=direct),
        grid=(b // rows, s // chunk),
        in_specs=[
            pl.BlockSpec((rows, chunk, qk_w), lambda i, j: (i, j, 0)),
            pl.BlockSpec((rows, chunk, qk_w), lambda i, j: (i, j, 1)),
            pl.BlockSpec((rows, chunk, v_w), lambda i, j: (i, j, 1)),
            pl.BlockSpec((rows, chunk, v_w), lambda i, j: (i, j, 2)),
            pl.BlockSpec((rows, chunk, LANE), lambda i, j: (i, j, 0)),
            pl.BlockSpec((LANE, qk_w), const),
            pl.BlockSpec((1, qk_w), const),
            pl.BlockSpec((1, v_w), const),
            pl.BlockSpec(seg_matrix.shape, const),
        ],
        out_specs=pl.BlockSpec((rows, chunk, v_w), lambda i, j: (i, j, 0)),
        out_shape=jax.ShapeDtypeStruct((b, s, v_w), BF16),
        scratch_shapes=[pltpu.VMEM((rows, heads, dk, dv), F32)],
        compiler_params=pltpu.CompilerParams(
            dimension_semantics=("parallel", "arbitrary"), vmem_limit_bytes=VMEM_LIMIT_BYTES),
        name="gla_mixer",
    )(main, main, main, main, z_low, w_up, b_up, out_gain, seg_matrix)


def _out_ffn_kernel(h_ref, x_ref, wo_ref, g_ref, w1_ref, w2_ref, fg_ref, o_ref, *, ff_chunk, final_norm):
    x1 = x_ref[...] + _dot(h_ref[...], wo_ref[...])
    hn = _rmsnorm(x1, g_ref[...]).astype(BF16)
    acc = x1
    for c in range(w1_ref.shape[1] // ff_chunk):
        u = jnp.maximum(_dot(hn, w1_ref[:, c * ff_chunk:(c + 1) * ff_chunk]), 0.0)
        acc = acc + _dot((u * u).astype(BF16), w2_ref[c * ff_chunk:(c + 1) * ff_chunk, :])
    if final_norm:
        acc = _rmsnorm(acc, fg_ref[...])
    o_ref[...] = acc


def _out_ffn(h2d, x2d, w_out_all, mixer_layer, norm_g, w1_all, w2_all, layer, final_g,
             *, final_norm, tm, ff_chunk):
    t, d = x2d.shape
    v_w, d_ff = h2d.shape[1], w1_all.shape[2]
    const = lambda i: (0, 0)
    resident = pl.Buffered(1)
    return pl.pallas_call(
        functools.partial(_out_ffn_kernel, ff_chunk=ff_chunk, final_norm=final_norm),
        grid=(t // tm,),
        in_specs=[
            pl.BlockSpec((tm, v_w), lambda i: (i, 0)),
            pl.BlockSpec((tm, d), lambda i: (i, 0)),
            pl.BlockSpec((None, v_w, d), lambda i: (mixer_layer, 0, 0), pipeline_mode=resident),
            pl.BlockSpec((1, d), const),
            pl.BlockSpec((None, d, d_ff), lambda i: (layer, 0, 0), pipeline_mode=resident),
            pl.BlockSpec((None, d_ff, d), lambda i: (layer, 0, 0), pipeline_mode=resident),
            pl.BlockSpec((1, d), const),
        ],
        out_specs=pl.BlockSpec((tm, d), lambda i: (i, 0)),
        out_shape=jax.ShapeDtypeStruct((t, d), F32),
        compiler_params=pltpu.CompilerParams(
            dimension_semantics=("parallel",), vmem_limit_bytes=VMEM_LIMIT_BYTES),
        name="out_proj_ffn",
    )(h2d, x2d, w_out_all, norm_g, w1_all, w2_all, final_g)


PROJ_ROWS = 1024
FFN_ROWS = 1024
FFN_CHUNK = 1024
MLSTM_CHUNK = 256
GLA_CHUNK = 128
MIXER_ROWS = 4
GLA_DIRECT = 8


def _pad_lanes(a):
    return jnp.pad(a, ((0, 0), (0, LANE - a.shape[1])))


def kernel(x, norm_mix_g, norm_ffn_g, final_norm_g, mlstm_w_in, mlstm_b_gate, mlstm_out_norm_g, mlstm_w_out,
           gla_w_in, gla_w_gate_up, gla_b_gate, gla_out_norm_g, gla_w_out, ffn_w1, ffn_w2):
    b, s, d = x.shape
    depth = norm_mix_g.shape[0]
    heads = N_HEADS
    qk_w, v_w = d // 2, d
    n_main = 2 * qk_w + 2 * v_w
    dk = qk_w // heads
    t = b * s
    x2d = x.reshape(t, d)
    final_g = final_norm_g.reshape(1, d)
    n_cols = n_main // qk_w

    mlstm_w_in_b, gla_w_in_b = mlstm_w_in.astype(BF16), gla_w_in.astype(BF16)
    mlstm_w_out_b, gla_w_out_b = mlstm_w_out.astype(BF16), gla_w_out.astype(BF16)
    ffn_w1_b, ffn_w2_b = ffn_w1.astype(BF16), ffn_w2.astype(BF16)

    for i in range(depth):
        j = i // 2
        norm_g = norm_mix_g[i].reshape(1, d)
        if i % 2 == 0:
            scales = (1.0, dk ** -0.5) + (1.0,) * (n_cols - 2)
            main, gates = _input_projection(
                x2d, norm_g, mlstm_w_in_b, j, n_main, _pad_lanes(mlstm_w_in[j][:, n_main:]).astype(BF16),
                _pad_lanes(mlstm_b_gate[j].reshape(1, -1)), scales, qk_w, PROJ_ROWS)
            mixed = _mlstm_mixer(main.reshape(b, s, n_main), gates.reshape(b, s, LANE),
                                 mlstm_out_norm_g[j].reshape(1, v_w),
                                 heads=heads, qk_w=qk_w, v_w=v_w, chunk=MLSTM_CHUNK, rows=MIXER_ROWS)
            w_out_b = mlstm_w_out_b
        else:
            scales = (dk ** -0.5,) + (1.0,) * (n_cols - 1)
            main, z_low = _input_projection(
                x2d, norm_g, gla_w_in_b, j, n_main, _pad_lanes(gla_w_in[j][:, n_main:]).astype(BF16),
                jnp.zeros((1, LANE), F32), scales, qk_w, PROJ_ROWS)
            rank = gla_w_gate_up.shape[1]
            w_up = jnp.pad(gla_w_gate_up[j], ((0, LANE - rank), (0, 0))).astype(BF16)
            mixed = _gla_mixer(main.reshape(b, s, n_main), z_low.reshape(b, s, LANE), w_up,
                               gla_b_gate[j].reshape(1, qk_w), gla_out_norm_g[j].reshape(1, v_w),
                               heads=heads, qk_w=qk_w, v_w=v_w, chunk=GLA_CHUNK, direct=GLA_DIRECT,
                               rows=MIXER_ROWS)
            w_out_b = gla_w_out_b
        x2d = _out_ffn(mixed.reshape(t, v_w), x2d, w_out_b, j, norm_ffn_g[i].reshape(1, d),
                       ffn_w1_b, ffn_w2_b, i, final_g,
                       final_norm=(i == depth - 1), tm=FFN_ROWS, ff_chunk=FFN_CHUNK)
    return x2d.reshape(b, s, d)
```

```python
import functools

import math

import numpy as np

import jax
import jax.numpy as jnp
from jax import lax
from jax.experimental import pallas as pl
from jax.experimental.pallas import tpu as pltpu

EPS = 1e-6
N_HEADS = 4
GLA_TAU = 16.0
LOG2_E = math.log2(math.e)

LANE = 128
SUBLANE = 8
VMEM_LIMIT_BYTES = 56 * 1024 * 1024

F32 = jnp.float32
BF16 = jnp.bfloat16

_NT = (((1,), (1,)), ((), ()))
_TN = (((0,), (0,)), ((), ()))


def _dot(a, b):
    return jnp.dot(a, b, preferred_element_type=F32)


def _dot_nt(a, b):
    return lax.dot_general(a, b, _NT, preferred_element_type=F32)


def _dot_tn(a, b):
    return lax.dot_general(a, b, _TN, preferred_element_type=F32)


def _rmsnorm(x, g):
    return x * lax.rsqrt(jnp.mean(x * x, axis=-1, keepdims=True) + EPS) * g


def _log_sigmoid(x):
    return jnp.minimum(x, 0.0) - jnp.log(1.0 + jnp.exp(-jnp.abs(x)))


def _cumsum_matrix(chunk):
    tril = np.tril(np.ones((chunk, chunk), np.float32))
    return jnp.asarray(np.concatenate([tril] * 3, axis=1), dtype=BF16)


def _cumsum_rows(tril3, x):
    hi = x.astype(BF16)
    r1 = x - hi.astype(F32)
    mid = r1.astype(BF16)
    lo = (r1 - mid.astype(F32)).astype(BF16)
    return _dot(tril3, jnp.concatenate([hi, mid, lo], axis=0))


def _proj_kernel(x_ref, g_ref, w_ref, wg_ref, bg_ref, main_ref, gate_ref, *, col_scales, tn):
    xb = _rmsnorm(x_ref[...], g_ref[...]).astype(BF16)
    for c, scale in enumerate(col_scales):
        acc = _dot(xb, w_ref[:, c * tn:(c + 1) * tn])
        if scale != 1.0:
            acc = acc * scale
        main_ref[:, c * tn:(c + 1) * tn] = acc.astype(BF16)
    gate_ref[...] = _dot(xb, wg_ref[...]) + bg_ref[...]


def _input_projection(x2d, norm_g, w_in_all, layer, n_main, w_gate, b_gate, col_scales, tn, tm):
    t, d = x2d.shape
    const = lambda i: (0, 0)
    return pl.pallas_call(
        functools.partial(_proj_kernel, col_scales=col_scales, tn=tn),
        grid=(t // tm,),
        in_specs=[
            pl.BlockSpec((tm, d), lambda i: (i, 0)),
            pl.BlockSpec((1, d), const),
            pl.BlockSpec((None, d, n_main), lambda i: (layer, 0, 0)),
            pl.BlockSpec((d, LANE), const),
            pl.BlockSpec((1, LANE), const),
        ],
        out_specs=[
            pl.BlockSpec((tm, n_main), lambda i: (i, 0)),
            pl.BlockSpec((tm, LANE), lambda i: (i, 0)),
        ],
        out_shape=[
            jax.ShapeDtypeStruct((t, n_main), BF16),
            jax.ShapeDtypeStruct((t, LANE), F32),
        ],
        compiler_params=pltpu.CompilerParams(
            dimension_semantics=("parallel",), vmem_limit_bytes=VMEM_LIMIT_BYTES),
        name="input_projection",
    )(x2d, norm_g, w_in_all, w_gate, b_gate)


def _cummax_rows(x):
    L = x.shape[0]
    tiles = L // SUBLANE
    x3 = x.reshape(tiles, SUBLANE, LANE)
    sub = lax.broadcasted_iota(jnp.int32, x3.shape, 1)
    shift = 1
    while shift < SUBLANE:
        x3 = jnp.maximum(x3, jnp.where(sub >= shift, pltpu.roll(x3, shift, 1), -jnp.inf))
        shift *= 2
    tot = jnp.broadcast_to(x3[:, SUBLANE - 1:SUBLANE, :], x3.shape)
    before = jnp.concatenate([jnp.full((1, SUBLANE, LANE), -jnp.inf, F32), tot[:tiles - 1]], axis=0)
    shift = 1
    while shift < tiles:
        pad = jnp.full((shift, SUBLANE, LANE), -jnp.inf, F32)
        before = jnp.maximum(before, jnp.concatenate([pad, before[:tiles - shift]], axis=0))
        shift *= 2
    return jnp.maximum(x3, before).reshape(L, LANE)


def _mlstm_kernel(q_ref, k_ref, v_ref, o_ref, gt_ref, gain_ref, tril_ref, out_ref, c_ref, m_ref,
                  *, heads, dk, dv, chunk):
    L = chunk

    @pl.when(pl.program_id(1) == 0)
    def _():
        c_ref[...] = jnp.zeros_like(c_ref)
        m_ref[...] = jnp.zeros_like(m_ref)

    causal = lax.broadcasted_iota(jnp.int32, (L, L), 1) <= lax.broadcasted_iota(jnp.int32, (L, L), 0)
    lane = lax.broadcasted_iota(jnp.int32, (L, LANE), 1)
    head_lanes = lane < heads
    ones_col = [jnp.where(lane == h, 1.0, 0.0).astype(BF16) for h in range(heads)]

    def gate_terms(bi):
        gates = gt_ref[bi]
        pre = jnp.where(head_lanes, gates, _log_sigmoid(gates)) * LOG2_E
        cum = _cumsum_rows(tril_ref[...], pre)
        rows_t = jnp.where(head_lanes, pre, cum).T
        li = jnp.where(head_lanes, pre, 0.0)
        b = jnp.where(head_lanes, pltpu.roll(cum, LANE - heads, 1), 0.0)
        m_prev = m_ref[bi]
        g_tot = b[L - 1:L, :]

        mm = jnp.maximum(m_prev, _cummax_rows(li - b))
        w_inter = jnp.exp2(m_prev - mm)

        a = g_tot - b + li
        m_new = jnp.maximum(g_tot + m_prev, jnp.max(a, axis=0, keepdims=True))
        w_s = jnp.exp2(a - m_new)
        decay = jnp.exp2(g_tot + m_prev - m_new)
        m_ref[bi] = m_new
        return rows_t, b, mm, w_inter, w_s, decay

    n_rows = q_ref.shape[0]
    pending = None
    next_terms = gate_terms(0)
    for bi in range(n_rows):
        rows_t, b, mm, w_inter, w_s, decay = next_terms
        if bi + 1 < n_rows:
            next_terms = gate_terms(bi + 1)

        nums, den = [], jnp.zeros((L, LANE), F32)
        for h in range(heads):
            q = q_ref[bi, :, h * dk:(h + 1) * dk]
            k = k_ref[bi, :, h * dk:(h + 1) * dk]
            v_ext = jnp.concatenate([v_ref[bi, :, h * dv:(h + 1) * dv], ones_col[h]], axis=1)
            c_row = rows_t[h:h + 1, :] - rows_t[heads + h:heads + h + 1, :]
            p = jnp.where(causal, jnp.exp2(c_row - mm[:, h:h + 1]), 0.0)
            s = (_dot_nt(q, k) * p).astype(BF16)
            q_w = (q.astype(F32) * w_inter[:, h:h + 1]).astype(BF16)
            state = c_ref[bi, h]
            num = _dot(q_w, state.astype(BF16)) + _dot(s, v_ext)
            nums.append(num[:, :dv])
            den = den + num[:, dv:]

            kw = (k.astype(F32) * w_s[:, h:h + 1]).astype(BF16)
            c_ref[bi, h] = decay[:, h:h + 1] * state + _dot_tn(kw, v_ext)

            if pending is not None:
                _mlstm_epilogue(o_ref, gain_ref, out_ref, *pending, h, dv)

        inv = 1.0 / jnp.maximum(jnp.abs(den), jnp.exp2(-(b + mm)))
        pending = (bi, nums, inv)
    for h in range(heads):
        _mlstm_epilogue(o_ref, gain_ref, out_ref, *pending, h, dv)


def _mlstm_epilogue(o_ref, gain_ref, out_ref, bi, nums, inv, h, dv):
    cols = slice(h * dv, (h + 1) * dv)
    gated = jax.nn.sigmoid(o_ref[bi, :, cols].astype(F32)) * (nums[h] * inv[:, h:h + 1])
    out_ref[bi, :, cols] = _rmsnorm(gated, gain_ref[:, cols]).astype(BF16)


def _mlstm_mixer(main, gates, out_gain, *, heads, qk_w, v_w, chunk, rows):
    b, s, _ = main.shape
    assert 2 * qk_w == v_w and s % chunk == 0 and b % rows == 0 and 2 * heads <= LANE
    dk, dv = qk_w // heads, v_w // heads
    tril3 = _cumsum_matrix(chunk)
    const = lambda i, j: (0, 0)
    return pl.pallas_call(
        functools.partial(_mlstm_kernel, heads=heads, dk=dk, dv=dv, chunk=chunk),
        grid=(b // rows, s // chunk),
        in_specs=[
            pl.BlockSpec((rows, chunk, qk_w), lambda i, j: (i, j, 0)),
            pl.BlockSpec((rows, chunk, qk_w), lambda i, j: (i, j, 1)),
            pl.BlockSpec((rows, chunk, v_w), lambda i, j: (i, j, 1)),
            pl.BlockSpec((rows, chunk, v_w), lambda i, j: (i, j, 2)),
            pl.BlockSpec((rows, chunk, LANE), lambda i, j: (i, j, 0)),
            pl.BlockSpec((1, v_w), const),
            pl.BlockSpec(tril3.shape, const),
        ],
        out_specs=pl.BlockSpec((rows, chunk, v_w), lambda i, j: (i, j, 0)),
        out_shape=jax.ShapeDtypeStruct((b, s, v_w), BF16),
        scratch_shapes=[
            pltpu.VMEM((rows, heads, dk, dv + LANE), F32),
            pltpu.VMEM((rows, 1, LANE), F32),
        ],
        compiler_params=pltpu.CompilerParams(
            dimension_semantics=("parallel", "arbitrary"), vmem_limit_bytes=VMEM_LIMIT_BYTES),
        name="mlstm_mixer",
    )(main, main, main, main, gates, out_gain, tril3)


def _gla_levels(chunk, direct):
    levels, w = [], direct
    while w < chunk:
        levels.append(w)
        w *= 2
    return tuple(levels)


def _gla_segment_matrix(chunk, direct):
    j = np.arange(chunk)[:, None]
    s = np.arange(chunk)[None, :]
    blocks = [s <= j]
    for w in _gla_levels(chunk, direct):
        mid = (j // (2 * w)) * (2 * w) + w - 1
        upper = (j % (2 * w)) >= w
        blocks.append(np.where(upper, (s > mid) & (s <= j), (s > j) & (s <= mid)))
    seg = np.concatenate(blocks, axis=0).astype(np.float32)
    return jnp.asarray(np.concatenate([seg, seg], axis=1), dtype=BF16)


def _gla_kernel(q_ref, k_ref, v_ref, r_ref, zl_ref, wup_ref, bup_ref, gain_ref, seg_ref, out_ref, s_ref,
                *, heads, dk, dv, chunk, direct):
    L = chunk
    levels = _gla_levels(L, direct)

    @pl.when(pl.program_id(1) == 0)
    def _():
        s_ref[...] = jnp.zeros_like(s_ref)

    row = lax.broadcasted_iota(jnp.int32, (L, L), 0)
    col = lax.broadcasted_iota(jnp.int32, (L, L), 1)
    diff = row ^ col
    level_idx = sum(jnp.where(diff >= w, 1, 0) for w in levels)
    code = jnp.where(col > row, -1, jnp.where(diff < direct, row - col, direct - 1 + level_idx))
    pair_masks = [code == c for c in range(direct + len(levels))]
    row_d = lax.broadcasted_iota(jnp.int32, (L, dk), 0)
    upper_masks = [(row_d & w) != 0 for w in levels]

    def decays(bi):
        z = _dot(zl_ref[bi].astype(BF16), wup_ref[...]) + bup_ref[...]
        la = _log_sigmoid(z) * (LOG2_E / GLA_TAU)
        la_hi = la.astype(BF16)
        la_lo = (la - la_hi.astype(F32)).astype(BF16)
        seg = _dot(seg_ref[...], jnp.concatenate([la_hi, la_lo], axis=0))
        return jnp.exp2(la), seg

    def matrix_stage(bi, h, step, seg):
        lanes = slice(h * dk, (h + 1) * dk)
        b = seg[0:L, lanes]
        g_tot = b[L - 1:L, :]
        qf = q_ref[bi, :, lanes].astype(F32)
        kf = k_ref[bi, :, lanes].astype(F32)
        v = v_ref[bi, :, h * dv:(h + 1) * dv]
        s_prev = s_ref[bi, h]
        o_state = _dot((qf * jnp.exp2(b)).astype(BF16), s_prev.astype(BF16))

        level_products = []
        for i, w in enumerate(levels):
            decay = jnp.exp2(seg[(1 + i) * L:(2 + i) * L, lanes])
            mixed = (jnp.where(upper_masks[i], qf, kf) * decay).astype(BF16)
            level_products.append(_dot_nt(mixed, mixed))

        kg = (kf * jnp.exp2(g_tot - b)).astype(BF16)
        g_col = jnp.broadcast_to(jnp.exp2(g_tot), (LANE, dk)).T
        g_col = jnp.concatenate([g_col] * (dv // LANE), axis=1)
        s_ref[bi, h] = g_col * s_prev + _dot_tn(kg, v)
        return bi, h, qf, kf, step[:, lanes], o_state, level_products

    def vector_stage(bi, h, qf, kf, step_h, o_state, level_products):
        a_mat = jnp.zeros((L, L), F32)
        q3 = qf.reshape(L // SUBLANE, SUBLANE, dk)
        step3 = step_h.reshape(L // SUBLANE, SUBLANE, dk)
        y = kf.reshape(L // SUBLANE, SUBLANE, dk)
        for t in range(direct):
            if t > 0:
                y = step3 * pltpu.roll(y, 1, 1)
            r = jnp.sum(q3 * y, axis=-1, keepdims=True).reshape(L, 1)
            a_mat = jnp.where(pair_masks[t], r, a_mat)
        for i in range(len(levels)):
            a_mat = jnp.where(pair_masks[direct + i], level_products[i], a_mat)
        return bi, h, o_state + _dot(a_mat.astype(BF16), v_ref[bi, :, h * dv:(h + 1) * dv])

    n_rows = q_ref.shape[0]
    in_matrix = in_vector = None
    next_decays = decays(0)
    for bi in range(n_rows):
        step, seg = next_decays
        if bi + 1 < n_rows:
            next_decays = decays(bi + 1)
        for h in range(heads):
            new_matrix = matrix_stage(bi, h, step, seg)
            new_vector = vector_stage(*in_matrix) if in_matrix is not None else None
            if in_vector is not None:
                _gla_epilogue(r_ref, gain_ref, out_ref, *in_vector, dv)
            in_matrix, in_vector = new_matrix, new_vector
    last_vector = vector_stage(*in_matrix)
    if in_vector is not None:
        _gla_epilogue(r_ref, gain_ref, out_ref, *in_vector, dv)
    _gla_epilogue(r_ref, gain_ref, out_ref, *last_vector, dv)


def _gla_epilogue(r_ref, gain_ref, out_ref, bi, h, o, dv):
    cols = slice(h * dv, (h + 1) * dv)
    rr = r_ref[bi, :, cols].astype(F32)
    out_ref[bi, :, cols] = (rr * jax.nn.sigmoid(rr) * _rmsnorm(o, gain_ref[:, cols])).astype(BF16)


def _gla_mixer(main, z_low, w_up, b_up, out_gain, *, heads, qk_w, v_w, chunk, direct, rows):
    b, s, _ = main.shape
    assert 2 * qk_w == v_w and s % chunk == 0 and SUBLANE % direct == 0 and b % rows == 0
    dk, dv = qk_w // heads, v_w // heads
    seg_matrix = _gla_segment_matrix(chunk, direct)
    const = lambda i, j: (0, 0)
    return pl.pallas_call(
        functools.partial(_gla_kernel, heads=heads, dk=dk, dv=dv, chunk=chunk, direct=direct),
        grid=(b // rows, s // chunk),
        in_specs=[
            pl.BlockSpec((rows, chunk, qk_w), lambda i, j: (i, j, 0)),
            pl.BlockSpec((rows, chunk, qk_w), lambda i, j: (i, j, 1)),
            pl.BlockSpec((rows, chunk, v_w), lambda i, j: (i, j, 1)),
            pl.BlockSpec((rows, chunk, v_w), lambda i, j: (i, j, 2)),
            pl.BlockSpec((rows, chunk, LANE), lambda i, j: (i, j, 0)),
            pl.BlockSpec((LANE, qk_w), const),
            pl.BlockSpec((1, qk_w), const),
            pl.BlockSpec((1, v_w), const),
            pl.BlockSpec(seg_matrix.shape, const),
        ],
        out_specs=pl.BlockSpec((rows, chunk, v_w), lambda i, j: (i, j, 0)),
        out_shape=jax.ShapeDtypeStruct((b, s, v_w), BF16),
        scratch_shapes=[pltpu.VMEM((rows, heads, dk, dv), F32)],
        compiler_params=pltpu.CompilerParams(
            dimension_semantics=("parallel", "arbitrary"), vmem_limit_bytes=VMEM_LIMIT_BYTES),
        name="gla_mixer",
    )(main, main, main, main, z_low, w_up, b_up, out_gain, seg_matrix)


def _out_ffn_kernel(h_ref, x_ref, wo_ref, g_ref, w1_ref, w2_ref, fg_ref, o_ref, *, ff_chunk, final_norm):
    x1 = x_ref[...] + _dot(h_ref[...], wo_ref[...])
    hn = _rmsnorm(x1, g_ref[...]).astype(BF16)
    acc = x1
    for c in range(w1_ref.shape[1] // ff_chunk):
        u = jnp.maximum(_dot(hn, w1_ref[:, c * ff_chunk:(c + 1) * ff_chunk]), 0.0)
        acc = acc + _dot((u * u).astype(BF16), w2_ref[c * ff_chunk:(c + 1) * ff_chunk, :])
    if final_norm:
        acc = _rmsnorm(acc, fg_ref[...])
    o_ref[...] = acc


def _out_ffn(h2d, x2d, w_out_all, mixer_layer, norm_g, w1_all, w2_all, layer, final_g,
             *, final_norm, tm, ff_chunk):
    t, d = x2d.shape
    v_w, d_ff = h2d.shape[1], w1_all.shape[2]
    const = lambda i: (0, 0)
    resident = pl.Buffered(1)
    return pl.pallas_call(
        functools.partial(_out_ffn_kernel, ff_chunk=ff_chunk, final_norm=final_norm),
        grid=(t // tm,),
        in_specs=[
            pl.BlockSpec((tm, v_w), lambda i: (i, 0)),
            pl.BlockSpec((tm, d), lambda i: (i, 0)),
            pl.BlockSpec((None, v_w, d), lambda i: (mixer_layer, 0, 0), pipeline_mode=resident),
            pl.BlockSpec((1, d), const),
            pl.BlockSpec((None, d, d_ff), lambda i: (layer, 0, 0), pipeline_mode=resident),
            pl.BlockSpec((None, d_ff, d), lambda i: (layer, 0, 0), pipeline_mode=resident),
            pl.BlockSpec((1, d), const),
        ],
        out_specs=pl.BlockSpec((tm, d), lambda i: (i, 0)),
        out_shape=jax.ShapeDtypeStruct((t, d), F32),
        compiler_params=pltpu.CompilerParams(
            dimension_semantics=("parallel",), vmem_limit_bytes=VMEM_LIMIT_BYTES),
        name="out_proj_ffn",
    )(h2d, x2d, w_out_all, norm_g, w1_all, w2_all, final_g)


PROJ_ROWS = 1024
FFN_ROWS = 1024
FFN_CHUNK = 1024
MLSTM_CHUNK = 256
GLA_CHUNK = 128
MIXER_ROWS = 8
GLA_DIRECT = 8


def _pad_lanes(a):
    return jnp.pad(a, ((0, 0), (0, LANE - a.shape[1])))


def kernel(x, norm_mix_g, norm_ffn_g, final_norm_g, mlstm_w_in, mlstm_b_gate, mlstm_out_norm_g, mlstm_w_out,
           gla_w_in, gla_w_gate_up, gla_b_gate, gla_out_norm_g, gla_w_out, ffn_w1, ffn_w2):
    b, s, d = x.shape
    depth = norm_mix_g.shape[0]
    heads = N_HEADS
    qk_w, v_w = d // 2, d
    n_main = 2 * qk_w + 2 * v_w
    dk = qk_w // heads
    t = b * s
    x2d = x.reshape(t, d)
    final_g = final_norm_g.reshape(1, d)
    n_cols = n_main // qk_w

    mlstm_w_in_b, gla_w_in_b = mlstm_w_in.astype(BF16), gla_w_in.astype(BF16)
    mlstm_w_out_b, gla_w_out_b = mlstm_w_out.astype(BF16), gla_w_out.astype(BF16)
    ffn_w1_b, ffn_w2_b = ffn_w1.astype(BF16), ffn_w2.astype(BF16)

    for i in range(depth):
        j = i // 2
        norm_g = norm_mix_g[i].reshape(1, d)
        if i % 2 == 0:
            scales = (1.0, dk ** -0.5) + (1.0,) * (n_cols - 2)
            main, gates = _input_projection(
                x2d, norm_g, mlstm_w_in_b, j, n_main, _pad_lanes(mlstm_w_in[j][:, n_main:]).astype(BF16),
                _pad_lanes(mlstm_b_gate[j].reshape(1, -1)), scales, qk_w, PROJ_ROWS)
            mixed = _mlstm_mixer(main.reshape(b, s, n_main), gates.reshape(b, s, LANE),
                                 mlstm_out_norm_g[j].reshape(1, v_w),
                                 heads=heads, qk_w=qk_w, v_w=v_w, chunk=MLSTM_CHUNK, rows=MIXER_ROWS)
            w_out_b = mlstm_w_out_b
        else:
            scales = (dk ** -0.5,) + (1.0,) * (n_cols - 1)
            main, z_low = _input_projection(
                x2d, norm_g, gla_w_in_b, j, n_main, _pad_lanes(gla_w_in[j][:, n_main:]).astype(BF16),
                jnp.zeros((1, LANE), F32), scales, qk_w, PROJ_ROWS)
            rank = gla_w_gate_up.shape[1]
            w_up = jnp.pad(gla_w_gate_up[j], ((0, LANE - rank), (0, 0))).astype(BF16)
            mixed = _gla_mixer(main.reshape(b, s, n_main), z_low.reshape(b, s, LANE), w_up,
                               gla_b_gate[j].reshape(1, qk_w), gla_out_norm_g[j].reshape(1, v_w),
                               heads=heads, qk_w=qk_w, v_w=v_w, chunk=GLA_CHUNK, direct=GLA_DIRECT,
                               rows=MIXER_ROWS)
            w_out_b = gla_w_out_b
        x2d = _out_ffn(mixed.reshape(t, v_w), x2d, w_out_b, j, norm_ffn_g[i].reshape(1, d),
                       ffn_w1_b, ffn_w2_b, i, final_g,
                       final_norm=(i == depth - 1), tm=FFN_ROWS, ff_chunk=FFN_CHUNK)
    return x2d.reshape(b, s, d)
```

```python
import functools

import math

import numpy as np

import jax
import jax.numpy as jnp
from jax import lax
from jax.experimental import pallas as pl
from jax.experimental.pallas import tpu as pltpu

EPS = 1e-6
N_HEADS = 4
GLA_TAU = 16.0
LOG2_E = math.log2(math.e)

LANE = 128
SUBLANE = 8
VMEM_LIMIT_BYTES = 56 * 1024 * 1024

F32 = jnp.float32
BF16 = jnp.bfloat16

_NT = (((1,), (1,)), ((), ()))
_TN = (((0,), (0,)), ((), ()))


def _dot(a, b):
    return jnp.dot(a, b, preferred_element_type=F32)


def _dot_nt(a, b):
    return lax.dot_general(a, b, _NT, preferred_element_type=F32)


def _dot_tn(a, b):
    return lax.dot_general(a, b, _TN, preferred_element_type=F32)


def _rmsnorm(x, g):
    return x * lax.rsqrt(jnp.mean(x * x, axis=-1, keepdims=True) + EPS) * g


def _log_sigmoid(x):
    return jnp.minimum(x, 0.0) - jnp.log(1.0 + jnp.exp(-jnp.abs(x)))


def _cumsum_matrix(chunk):
    tril = np.tril(np.ones((chunk, chunk), np.float32))
    return jnp.asarray(np.concatenate([tril] * 3, axis=1), dtype=BF16)


def _cumsum_rows(tril3, x):
    hi = x.astype(BF16)
    r1 = x - hi.astype(F32)
    mid = r1.astype(BF16)
    lo = (r1 - mid.astype(F32)).astype(BF16)
    return _dot(tril3, jnp.concatenate([hi, mid, lo], axis=0))


def _proj_kernel(x_ref, g_ref, w_ref, wg_ref, bg_ref, *out_refs, col_scales, tn):
    *wide_refs, gate_ref = out_refs
    targets = [(ref, c * tn) for ref in wide_refs for c in range(ref.shape[1] // tn)]
    xb = _rmsnorm(x_ref[...], g_ref[...]).astype(BF16)
    for c, ((ref, start), scale) in enumerate(zip(targets, col_scales)):
        acc = _dot(xb, w_ref[:, c * tn:(c + 1) * tn])
        if scale != 1.0:
            acc = acc * scale
        ref[:, start:start + tn] = acc.astype(BF16)
    gate_ref[...] = _dot(xb, wg_ref[...]) + bg_ref[...]


def _input_projection(x2d, norm_g, w_in_all, layer, widths, w_gate, b_gate, col_scales, tn, tm):
    t, d = x2d.shape
    n_main = sum(widths)
    assert len(col_scales) * tn == n_main and all(w % tn == 0 for w in widths)
    const = lambda i: (0, 0)
    return pl.pallas_call(
        functools.partial(_proj_kernel, col_scales=col_scales, tn=tn),
        grid=(t // tm,),
        in_specs=[
            pl.BlockSpec((tm, d), lambda i: (i, 0)),
            pl.BlockSpec((1, d), const),
            pl.BlockSpec((None, d, n_main), lambda i: (layer, 0, 0)),
            pl.BlockSpec((d, LANE), const),
            pl.BlockSpec((1, LANE), const),
        ],
        out_specs=[pl.BlockSpec((tm, w), lambda i: (i, 0)) for w in widths]
        + [pl.BlockSpec((tm, LANE), lambda i: (i, 0))],
        out_shape=[jax.ShapeDtypeStruct((t, w), BF16) for w in widths]
        + [jax.ShapeDtypeStruct((t, LANE), F32)],
        compiler_params=pltpu.CompilerParams(
            dimension_semantics=("parallel",), vmem_limit_bytes=VMEM_LIMIT_BYTES),
        name="input_projection",
    )(x2d, norm_g, w_in_all, w_gate, b_gate)


def _cummax_rows(x):
    L = x.shape[0]
    tiles = L // SUBLANE
    x3 = x.reshape(tiles, SUBLANE, LANE)
    sub = lax.broadcasted_iota(jnp.int32, x3.shape, 1)
    shift = 1
    while shift < SUBLANE:
        x3 = jnp.maximum(x3, jnp.where(sub >= shift, pltpu.roll(x3, shift, 1), -jnp.inf))
        shift *= 2
    tot = jnp.broadcast_to(x3[:, SUBLANE - 1:SUBLANE, :], x3.shape)
    before = jnp.concatenate([jnp.full((1, SUBLANE, LANE), -jnp.inf, F32), tot[:tiles - 1]], axis=0)
    shift = 1
    while shift < tiles:
        pad = jnp.full((shift, SUBLANE, LANE), -jnp.inf, F32)
        before = jnp.maximum(before, jnp.concatenate([pad, before[:tiles - shift]], axis=0))
        shift *= 2
    return jnp.maximum(x3, before).reshape(L, LANE)


def _mlstm_kernel(q_ref, k_ref, v_ref, o_ref, gt_ref, gain_ref, tril_ref, out_ref, c_ref, m_ref,
                  *, heads, dk, dv, chunk):
    L = chunk

    @pl.when(pl.program_id(1) == 0)
    def _():
        c_ref[...] = jnp.zeros_like(c_ref)
        m_ref[...] = jnp.zeros_like(m_ref)

    causal = lax.broadcasted_iota(jnp.int32, (L, L), 1) <= lax.broadcasted_iota(jnp.int32, (L, L), 0)
    lane = lax.broadcasted_iota(jnp.int32, (L, LANE), 1)
    head_lanes = lane < heads
    ones_col = [jnp.where(lane == h, 1.0, 0.0).astype(BF16) for h in range(heads)]

    def gate_terms(bi):
        gates = gt_ref[bi]
        pre = jnp.where(head_lanes, gates, _log_sigmoid(gates)) * LOG2_E
        cum = _cumsum_rows(tril_ref[...], pre)
        rows_t = jnp.where(head_lanes, pre, cum).T
        li = jnp.where(head_lanes, pre, 0.0)
        b = jnp.where(head_lanes, pltpu.roll(cum, LANE - heads, 1), 0.0)
        m_prev = m_ref[bi]
        g_tot = b[L - 1:L, :]

        mm = jnp.maximum(m_prev, _cummax_rows(li - b))
        w_inter = jnp.exp2(m_prev - mm)

        a = g_tot - b + li
        m_new = jnp.maximum(g_tot + m_prev, jnp.max(a, axis=0, keepdims=True))
        w_s = jnp.exp2(a - m_new)
        decay = jnp.exp2(g_tot + m_prev - m_new)
        m_ref[bi] = m_new
        return rows_t, b, mm, w_inter, w_s, decay

    n_rows = q_ref.shape[0]
    pending = None
    next_terms = gate_terms(0)
    for bi in range(n_rows):
        rows_t, b, mm, w_inter, w_s, decay = next_terms
        if bi + 1 < n_rows:
            next_terms = gate_terms(bi + 1)

        nums, den = [], jnp.zeros((L, LANE), F32)
        for h in range(heads):
            q = q_ref[bi, :, h * dk:(h + 1) * dk]
            k = k_ref[bi, :, h * dk:(h + 1) * dk]
            v_ext = jnp.concatenate([v_ref[bi, :, h * dv:(h + 1) * dv], ones_col[h]], axis=1)
            c_row = rows_t[h:h + 1, :] - rows_t[heads + h:heads + h + 1, :]
            p = jnp.where(causal, jnp.exp2(c_row - mm[:, h:h + 1]), 0.0)
            s = (_dot_nt(q, k) * p).astype(BF16)
            q_w = (q.astype(F32) * w_inter[:, h:h + 1]).astype(BF16)
            state = c_ref[bi, h]
            num = _dot(q_w, state.astype(BF16)) + _dot(s, v_ext)
            nums.append(num[:, :dv])
            den = den + num[:, dv:]

            kw = (k.astype(F32) * w_s[:, h:h + 1]).astype(BF16)
            c_ref[bi, h] = decay[:, h:h + 1] * state + _dot_tn(kw, v_ext)

            if pending is not None:
                _mlstm_epilogue(o_ref, gain_ref, out_ref, *pending, h, dv)

        inv = 1.0 / jnp.maximum(jnp.abs(den), jnp.exp2(-(b + mm)))
        pending = (bi, nums, inv)
    for h in range(heads):
        _mlstm_epilogue(o_ref, gain_ref, out_ref, *pending, h, dv)


def _mlstm_epilogue(o_ref, gain_ref, out_ref, bi, nums, inv, h, dv):
    cols = slice(h * dv, (h + 1) * dv)
    gated = jax.nn.sigmoid(o_ref[bi, :, cols].astype(F32)) * (nums[h] * inv[:, h:h + 1])
    out_ref[bi, :, cols] = _rmsnorm(gated, gain_ref[:, cols]).astype(BF16)


def _mlstm_mixer(q, k, v, o_gate, gates, out_gain, *, heads, chunk, rows):
    b, s, qk_w = q.shape
    v_w = v.shape[2]
    assert s % chunk == 0 and b % rows == 0 and 2 * heads <= LANE
    dk, dv = qk_w // heads, v_w // heads
    tril3 = _cumsum_matrix(chunk)
    const = lambda i, j: (0, 0)
    block = lambda i, j: (i, j, 0)
    return pl.pallas_call(
        functools.partial(_mlstm_kernel, heads=heads, dk=dk, dv=dv, chunk=chunk),
        grid=(b // rows, s // chunk),
        in_specs=[
            pl.BlockSpec((rows, chunk, qk_w), block),
            pl.BlockSpec((rows, chunk, qk_w), block),
            pl.BlockSpec((rows, chunk, v_w), block),
            pl.BlockSpec((rows, chunk, v_w), block),
            pl.BlockSpec((rows, chunk, LANE), block),
            pl.BlockSpec((1, v_w), const),
            pl.BlockSpec(tril3.shape, const),
        ],
        out_specs=pl.BlockSpec((rows, chunk, v_w), lambda i, j: (i, j, 0)),
        out_shape=jax.ShapeDtypeStruct((b, s, v_w), BF16),
        scratch_shapes=[
            pltpu.VMEM((rows, heads, dk, dv + LANE), F32),
            pltpu.VMEM((rows, 1, LANE), F32),
        ],
        compiler_params=pltpu.CompilerParams(
            dimension_semantics=("parallel", "arbitrary"), vmem_limit_bytes=VMEM_LIMIT_BYTES),
        name="mlstm_mixer",
    )(q, k, v, o_gate, gates, out_gain, tril3)


def _gla_levels(chunk, direct):
    levels, w = [], direct
    while w < chunk:
        levels.append(w)
        w *= 2
    return tuple(levels)


def _gla_segment_matrix(chunk, direct):
    j = np.arange(chunk)[:, None]
    s = np.arange(chunk)[None, :]
    blocks = [s <= j]
    for w in _gla_levels(chunk, direct):
        mid = (j // (2 * w)) * (2 * w) + w - 1
        upper = (j % (2 * w)) >= w
        blocks.append(np.where(upper, (s > mid) & (s <= j), (s > j) & (s <= mid)))
    seg = np.concatenate(blocks, axis=0).astype(np.float32)
    return jnp.asarray(np.concatenate([seg, seg], axis=1), dtype=BF16)


def _gla_kernel(q_ref, k_ref, v_ref, r_ref, zl_ref, wup_ref, bup_ref, gain_ref, seg_ref, out_ref, s_ref,
                *, heads, dk, dv, chunk, direct):
    L = chunk
    levels = _gla_levels(L, direct)

    @pl.when(pl.program_id(1) == 0)
    def _():
        s_ref[...] = jnp.zeros_like(s_ref)

    row = lax.broadcasted_iota(jnp.int32, (L, L), 0)
    col = lax.broadcasted_iota(jnp.int32, (L, L), 1)
    diff = row ^ col
    level_idx = sum(jnp.where(diff >= w, 1, 0) for w in levels)
    code = jnp.where(col > row, -1, jnp.where(diff < direct, row - col, direct - 1 + level_idx))
    pair_masks = [code == c for c in range(direct + len(levels))]
    row_d = lax.broadcasted_iota(jnp.int32, (L, dk), 0)
    upper_masks = [(row_d & w) != 0 for w in levels]

    def decays(bi):
        z = _dot(zl_ref[bi].astype(BF16), wup_ref[...]) + bup_ref[...]
        la = _log_sigmoid(z) * (LOG2_E / GLA_TAU)
        la_hi = la.astype(BF16)
        la_lo = (la - la_hi.astype(F32)).astype(BF16)
        seg = _dot(seg_ref[...], jnp.concatenate([la_hi, la_lo], axis=0))
        return jnp.exp2(la), seg

    def matrix_stage(bi, h, step, seg):
        lanes = slice(h * dk, (h + 1) * dk)
        b = seg[0:L, lanes]
        g_tot = b[L - 1:L, :]
        qf = q_ref[bi, :, lanes].astype(F32)
        kf = k_ref[bi, :, lanes].astype(F32)
        v = v_ref[bi, :, h * dv:(h + 1) * dv]
        s_prev = s_ref[bi, h]
        o_state = _dot((qf * jnp.exp2(b)).astype(BF16), s_prev.astype(BF16))

        level_products = []
        for i, w in enumerate(levels):
            decay = jnp.exp2(seg[(1 + i) * L:(2 + i) * L, lanes])
            mixed = (jnp.where(upper_masks[i], qf, kf) * decay).astype(BF16)
            level_products.append(_dot_nt(mixed, mixed))

        kg = (kf * jnp.exp2(g_tot - b)).astype(BF16)
        g_col = jnp.broadcast_to(jnp.exp2(g_tot), (LANE, dk)).T
        g_col = jnp.concatenate([g_col] * (dv // LANE), axis=1)
        s_ref[bi, h] = g_col * s_prev + _dot_tn(kg, v)
        return bi, h, qf, kf, step[:, lanes], o_state, level_products

    def vector_stage(bi, h, qf, kf, step_h, o_state, level_products):
        a_mat = jnp.zeros((L, L), F32)
        q3 = qf.reshape(L // SUBLANE, SUBLANE, dk)
        step3 = step_h.reshape(L // SUBLANE, SUBLANE, dk)
        y = kf.reshape(L // SUBLANE, SUBLANE, dk)
        for t in range(direct):
            if t > 0:
                y = step3 * pltpu.roll(y, 1, 1)
            r = jnp.sum(q3 * y, axis=-1, keepdims=True).reshape(L, 1)
            a_mat = jnp.where(pair_masks[t], r, a_mat)
        for i in range(len(levels)):
            a_mat = jnp.where(pair_masks[direct + i], level_products[i], a_mat)
        return bi, h, o_state + _dot(a_mat.astype(BF16), v_ref[bi, :, h * dv:(h + 1) * dv])

    n_rows = q_ref.shape[0]
    in_matrix = in_vector = None
    next_decays = decays(0)
    for bi in range(n_rows):
        step, seg = next_decays
        if bi + 1 < n_rows:
            next_decays = decays(bi + 1)
        for h in range(heads):
            new_matrix = matrix_stage(bi, h, step, seg)
            new_vector = vector_stage(*in_matrix) if in_matrix is not None else None
            if in_vector is not None:
                _gla_epilogue(r_ref, gain_ref, out_ref, *in_vector, dv)
            in_matrix, in_vector = new_matrix, new_vector
    last_vector = vector_stage(*in_matrix)
    if in_vector is not None:
        _gla_epilogue(r_ref, gain_ref, out_ref, *in_vector, dv)
    _gla_epilogue(r_ref, gain_ref, out_ref, *last_vector, dv)


def _gla_epilogue(r_ref, gain_ref, out_ref, bi, h, o, dv):
    cols = slice(h * dv, (h + 1) * dv)
    rr = r_ref[bi, :, cols].astype(F32)
    out_ref[bi, :, cols] = (rr * jax.nn.sigmoid(rr) * _rmsnorm(o, gain_ref[:, cols])).astype(BF16)


def _gla_mixer(q, k, v, r_gate, z_low, w_up, b_up, out_gain, *, heads, chunk, direct, rows):
    b, s, qk_w = q.shape
    v_w = v.shape[2]
    assert s % chunk == 0 and SUBLANE % direct == 0 and b % rows == 0
    dk, dv = qk_w // heads, v_w // heads
    seg_matrix = _gla_segment_matrix(chunk, direct)
    const = lambda i, j: (0, 0)
    block = lambda i, j: (i, j, 0)
    return pl.pallas_call(
        functools.partial(_gla_kernel, heads=heads, dk=dk, dv=dv, chunk=chunk, direct=direct),
        grid=(b // rows, s // chunk),
        in_specs=[
            pl.BlockSpec((rows, chunk, qk_w), block),
            pl.BlockSpec((rows, chunk, qk_w), block),
            pl.BlockSpec((rows, chunk, v_w), block),
            pl.BlockSpec((rows, chunk, v_w), block),
            pl.BlockSpec((rows, chunk, LANE), block),
            pl.BlockSpec((LANE, qk_w), const),
            pl.BlockSpec((1, qk_w), const),
            pl.BlockSpec((1, v_w), const),
            pl.BlockSpec(seg_matrix.shape, const),
        ],
        out_specs=pl.BlockSpec((rows, chunk, v_w), lambda i, j: (i, j, 0)),
        out_shape=jax.ShapeDtypeStruct((b, s, v_w), BF16),
        scratch_shapes=[pltpu.VMEM((rows, heads, dk, dv), F32)],
        compiler_params=pltpu.CompilerParams(
            dimension_semantics=("parallel", "arbitrary"), vmem_limit_bytes=VMEM_LIMIT_BYTES),
        name="gla_mixer",
    )(q, k, v, r_gate, z_low, w_up, b_up, out_gain, seg_matrix)


def _out_ffn_kernel(h_ref, x_ref, wo_ref, g_ref, w1_ref, w2_ref, fg_ref, o_ref, *, ff_chunk, final_norm):
    x1 = x_ref[...] + _dot(h_ref[...], wo_ref[...])
    hn = _rmsnorm(x1, g_ref[...]).astype(BF16)
    acc = x1
    for c in range(w1_ref.shape[1] // ff_chunk):
        u = jnp.maximum(_dot(hn, w1_ref[:, c * ff_chunk:(c + 1) * ff_chunk]), 0.0)
        acc = acc + _dot((u * u).astype(BF16), w2_ref[c * ff_chunk:(c + 1) * ff_chunk, :])
    if final_norm:
        acc = _rmsnorm(acc, fg_ref[...])
    o_ref[...] = acc


def _out_ffn(h2d, x2d, w_out_all, mixer_layer, norm_g, w1_all, w2_all, layer, final_g,
             *, final_norm, tm, ff_chunk):
    t, d = x2d.shape
    v_w, d_ff = h2d.shape[1], w1_all.shape[2]
    const = lambda i: (0, 0)
    resident = pl.Buffered(1)
    return pl.pallas_call(
        functools.partial(_out_ffn_kernel, ff_chunk=ff_chunk, final_norm=final_norm),
        grid=(t // tm,),
        in_specs=[
            pl.BlockSpec((tm, v_w), lambda i: (i, 0)),
            pl.BlockSpec((tm, d), lambda i: (i, 0)),
            pl.BlockSpec((None, v_w, d), lambda i: (mixer_layer, 0, 0), pipeline_mode=resident),
            pl.BlockSpec((1, d), const),
            pl.BlockSpec((None, d, d_ff), lambda i: (layer, 0, 0), pipeline_mode=resident),
            pl.BlockSpec((None, d_ff, d), lambda i: (layer, 0, 0), pipeline_mode=resident),
            pl.BlockSpec((1, d), const),
        ],
        out_specs=pl.BlockSpec((tm, d), lambda i: (i, 0)),
        out_shape=jax.ShapeDtypeStruct((t, d), F32),
        compiler_params=pltpu.CompilerParams(
            dimension_semantics=("parallel",), vmem_limit_bytes=VMEM_LIMIT_BYTES),
        name="out_proj_ffn",
    )(h2d, x2d, w_out_all, norm_g, w1_all, w2_all, final_g)


PROJ_ROWS = 1024
FFN_ROWS = 1024
FFN_CHUNK = 1024
MLSTM_CHUNK = 256
GLA_CHUNK = 128
MIXER_ROWS = 8
GLA_DIRECT = 8


def _pad_lanes(a):
    return jnp.pad(a, ((0, 0), (0, LANE - a.shape[1])))


def kernel(x, norm_mix_g, norm_ffn_g, final_norm_g, mlstm_w_in, mlstm_b_gate, mlstm_out_norm_g, mlstm_w_out,
           gla_w_in, gla_w_gate_up, gla_b_gate, gla_out_norm_g, gla_w_out, ffn_w1, ffn_w2):
    b, s, d = x.shape
    depth = norm_mix_g.shape[0]
    heads = N_HEADS
    qk_w, v_w = d // 2, d
    widths = (qk_w, qk_w, v_w, v_w)
    n_main = sum(widths)
    dk = qk_w // heads
    t = b * s
    x2d = x.reshape(t, d)
    final_g = final_norm_g.reshape(1, d)
    n_cols = n_main // qk_w

    mlstm_w_in_b, gla_w_in_b = mlstm_w_in.astype(BF16), gla_w_in.astype(BF16)
    mlstm_w_out_b, gla_w_out_b = mlstm_w_out.astype(BF16), gla_w_out.astype(BF16)
    ffn_w1_b, ffn_w2_b = ffn_w1.astype(BF16), ffn_w2.astype(BF16)

    for i in range(depth):
        j = i // 2
        norm_g = norm_mix_g[i].reshape(1, d)
        if i % 2 == 0:
            scales = (1.0, dk ** -0.5) + (1.0,) * (n_cols - 2)
            *wide, gates = _input_projection(
                x2d, norm_g, mlstm_w_in_b, j, widths, _pad_lanes(mlstm_w_in[j][:, n_main:]).astype(BF16),
                _pad_lanes(mlstm_b_gate[j].reshape(1, -1)), scales, qk_w, PROJ_ROWS)
            mixed = _mlstm_mixer(*(a.reshape(b, s, -1) for a in wide), gates.reshape(b, s, LANE),
                                 mlstm_out_norm_g[j].reshape(1, v_w),
                                 heads=heads, chunk=MLSTM_CHUNK, rows=MIXER_ROWS)
            w_out_b = mlstm_w_out_b
        else:
            scales = (dk ** -0.5,) + (1.0,) * (n_cols - 1)
            *wide, z_low = _input_projection(
                x2d, norm_g, gla_w_in_b, j, widths, _pad_lanes(gla_w_in[j][:, n_main:]).astype(BF16),
                jnp.zeros((1, LANE), F32), scales, qk_w, PROJ_ROWS)
            rank = gla_w_gate_up.shape[1]
            w_up = jnp.pad(gla_w_gate_up[j], ((0, LANE - rank), (0, 0))).astype(BF16)
            mixed = _gla_mixer(*(a.reshape(b, s, -1) for a in wide), z_low.reshape(b, s, LANE), w_up,
                               gla_b_gate[j].reshape(1, qk_w), gla_out_norm_g[j].reshape(1, v_w),
                               heads=heads, chunk=GLA_CHUNK, direct=GLA_DIRECT, rows=MIXER_ROWS)
            w_out_b = gla_w_out_b
        x2d = _out_ffn(mixed.reshape(t, v_w), x2d, w_out_b, j, norm_ffn_g[i].reshape(1, d),
                       ffn_w1_b, ffn_w2_b, i, final_g,
                       final_norm=(i == depth - 1), tm=FFN_ROWS, ff_chunk=FFN_CHUNK)
    return x2d.reshape(b, s, d)
```

```python
import functools

import math

import numpy as np

import jax
import jax.numpy as jnp
from jax import lax
from jax.experimental import pallas as pl
from jax.experimental.pallas import tpu as pltpu

EPS = 1e-6
N_HEADS = 4
GLA_TAU = 16.0
LOG2_E = math.log2(math.e)

LANE = 128
SUBLANE = 8
VMEM_LIMIT_BYTES = 56 * 1024 * 1024

F32 = jnp.float32
BF16 = jnp.bfloat16

_NT = (((1,), (1,)), ((), ()))
_TN = (((0,), (0,)), ((), ()))


def _dot(a, b):
    return jnp.dot(a, b, preferred_element_type=F32)


def _dot_nt(a, b):
    return lax.dot_general(a, b, _NT, preferred_element_type=F32)


def _dot_tn(a, b):
    return lax.dot_general(a, b, _TN, preferred_element_type=F32)


def _rmsnorm(x, g):
    return x * lax.rsqrt(jnp.mean(x * x, axis=-1, keepdims=True) + EPS) * g


def _log_sigmoid(x):
    return jnp.minimum(x, 0.0) - jnp.log(1.0 + jnp.exp(-jnp.abs(x)))


def _cumsum_matrix(chunk):
    tril = np.tril(np.ones((chunk, chunk), np.float32))
    return jnp.asarray(np.concatenate([tril] * 3, axis=1), dtype=BF16)


def _cumsum_rows(tril3, x):
    hi = x.astype(BF16)
    r1 = x - hi.astype(F32)
    mid = r1.astype(BF16)
    lo = (r1 - mid.astype(F32)).astype(BF16)
    return _dot(tril3, jnp.concatenate([hi, mid, lo], axis=0))


def _proj_kernel(x_ref, g_ref, w_ref, wg_ref, bg_ref, *out_refs, col_scales, tn):
    *wide_refs, gate_ref = out_refs
    targets = [(ref, c * tn) for ref in wide_refs for c in range(ref.shape[1] // tn)]
    xb = _rmsnorm(x_ref[...], g_ref[...]).astype(BF16)
    for c, ((ref, start), scale) in enumerate(zip(targets, col_scales)):
        acc = _dot(xb, w_ref[:, c * tn:(c + 1) * tn])
        if scale != 1.0:
            acc = acc * scale
        ref[:, start:start + tn] = acc.astype(BF16)
    gate_ref[...] = _dot(xb, wg_ref[...]) + bg_ref[...]


def _input_projection(x2d, norm_g, w_in_all, layer, widths, w_gate, b_gate, col_scales, tn, tm):
    t, d = x2d.shape
    n_main = sum(widths)
    assert len(col_scales) * tn == n_main and all(w % tn == 0 for w in widths)
    const = lambda i: (0, 0)
    return pl.pallas_call(
        functools.partial(_proj_kernel, col_scales=col_scales, tn=tn),
        grid=(t // tm,),
        in_specs=[
            pl.BlockSpec((tm, d), lambda i: (i, 0)),
            pl.BlockSpec((1, d), const),
            pl.BlockSpec((None, d, n_main), lambda i: (layer, 0, 0)),
            pl.BlockSpec((d, LANE), const),
            pl.BlockSpec((1, LANE), const),
        ],
        out_specs=[pl.BlockSpec((tm, w), lambda i: (i, 0)) for w in widths]
        + [pl.BlockSpec((tm, LANE), lambda i: (i, 0))],
        out_shape=[jax.ShapeDtypeStruct((t, w), BF16) for w in widths]
        + [jax.ShapeDtypeStruct((t, LANE), F32)],
        compiler_params=pltpu.CompilerParams(
            dimension_semantics=("parallel",), vmem_limit_bytes=VMEM_LIMIT_BYTES),
        name="input_projection",
    )(x2d, norm_g, w_in_all, w_gate, b_gate)


def _cummax_rows(x):
    L = x.shape[0]
    tiles = L // SUBLANE
    x3 = x.reshape(tiles, SUBLANE, LANE)
    sub = lax.broadcasted_iota(jnp.int32, x3.shape, 1)
    shift = 1
    while shift < SUBLANE:
        x3 = jnp.maximum(x3, jnp.where(sub >= shift, pltpu.roll(x3, shift, 1), -jnp.inf))
        shift *= 2
    tot = jnp.broadcast_to(x3[:, SUBLANE - 1:SUBLANE, :], x3.shape)
    before = jnp.concatenate([jnp.full((1, SUBLANE, LANE), -jnp.inf, F32), tot[:tiles - 1]], axis=0)
    shift = 1
    while shift < tiles:
        pad = jnp.full((shift, SUBLANE, LANE), -jnp.inf, F32)
        before = jnp.maximum(before, jnp.concatenate([pad, before[:tiles - shift]], axis=0))
        shift *= 2
    return jnp.maximum(x3, before).reshape(L, LANE)


def _mlstm_kernel(q_ref, k_ref, v_ref, gt_ref, tril_ref, out_ref, c_ref, m_ref,
                  *, heads, dk, dv, chunk):
    L = chunk

    @pl.when(pl.program_id(1) == 0)
    def _():
        c_ref[...] = jnp.zeros_like(c_ref)
        m_ref[...] = jnp.zeros_like(m_ref)

    causal = lax.broadcasted_iota(jnp.int32, (L, L), 1) <= lax.broadcasted_iota(jnp.int32, (L, L), 0)
    lane = lax.broadcasted_iota(jnp.int32, (L, LANE), 1)
    head_lanes = lane < heads
    ones_col = [jnp.where(lane == h, 1.0, 0.0).astype(BF16) for h in range(heads)]

    def gate_terms(bi):
        gates = gt_ref[bi]
        pre = jnp.where(head_lanes, gates, _log_sigmoid(gates)) * LOG2_E
        cum = _cumsum_rows(tril_ref[...], pre)
        rows_t = jnp.where(head_lanes, pre, cum).T
        li = jnp.where(head_lanes, pre, 0.0)
        b = jnp.where(head_lanes, pltpu.roll(cum, LANE - heads, 1), 0.0)
        m_prev = m_ref[bi]
        g_tot = b[L - 1:L, :]

        mm = jnp.maximum(m_prev, _cummax_rows(li - b))
        w_inter = jnp.exp2(m_prev - mm)

        a = g_tot - b + li
        m_new = jnp.maximum(g_tot + m_prev, jnp.max(a, axis=0, keepdims=True))
        w_s = jnp.exp2(a - m_new)
        decay = jnp.exp2(g_tot + m_prev - m_new)
        m_ref[bi] = m_new
        return rows_t, b, mm, w_inter, w_s, decay

    n_rows = q_ref.shape[0]
    pending = None
    next_terms = gate_terms(0)
    for bi in range(n_rows):
        rows_t, b, mm, w_inter, w_s, decay = next_terms
        if bi + 1 < n_rows:
            next_terms = gate_terms(bi + 1)

        nums, den = [], jnp.zeros((L, LANE), F32)
        for h in range(heads):
            q = q_ref[bi, :, h * dk:(h + 1) * dk]
            k = k_ref[bi, :, h * dk:(h + 1) * dk]
            v_ext = jnp.concatenate([v_ref[bi, :, h * dv:(h + 1) * dv], ones_col[h]], axis=1)
            c_row = rows_t[h:h + 1, :] - rows_t[heads + h:heads + h + 1, :]
            p = jnp.where(causal, jnp.exp2(c_row - mm[:, h:h + 1]), 0.0)
            s = (_dot_nt(q, k) * p).astype(BF16)
            q_w = (q.astype(F32) * w_inter[:, h:h + 1]).astype(BF16)
            state = c_ref[bi, h]
            num = _dot(q_w, state.astype(BF16)) + _dot(s, v_ext)
            nums.append(num[:, :dv])
            den = den + num[:, dv:]

            kw = (k.astype(F32) * w_s[:, h:h + 1]).astype(BF16)
            c_ref[bi, h] = decay[:, h:h + 1] * state + _dot_tn(kw, v_ext)

            if pending is not None:
                _mlstm_epilogue(out_ref, *pending, h, dv)

        inv = 1.0 / jnp.maximum(jnp.abs(den), jnp.exp2(-(b + mm)))
        pending = (bi, nums, inv)
    for h in range(heads):
        _mlstm_epilogue(out_ref, *pending, h, dv)


def _mlstm_epilogue(out_ref, bi, nums, inv, h, dv):
    out_ref[bi, :, h * dv:(h + 1) * dv] = (nums[h] * inv[:, h:h + 1]).astype(BF16)


def _mlstm_mixer(q, k, v, gates, *, heads, chunk, rows):
    b, s, qk_w = q.shape
    v_w = v.shape[2]
    assert s % chunk == 0 and b % rows == 0 and 2 * heads <= LANE
    dk, dv = qk_w // heads, v_w // heads
    tril3 = _cumsum_matrix(chunk)
    const = lambda i, j: (0, 0)
    block = lambda i, j: (i, j, 0)
    return pl.pallas_call(
        functools.partial(_mlstm_kernel, heads=heads, dk=dk, dv=dv, chunk=chunk),
        grid=(b // rows, s // chunk),
        in_specs=[
            pl.BlockSpec((rows, chunk, qk_w), block),
            pl.BlockSpec((rows, chunk, qk_w), block),
            pl.BlockSpec((rows, chunk, v_w), block),
            pl.BlockSpec((rows, chunk, LANE), block),
            pl.BlockSpec(tril3.shape, const),
        ],
        out_specs=pl.BlockSpec((rows, chunk, v_w), lambda i, j: (i, j, 0)),
        out_shape=jax.ShapeDtypeStruct((b, s, v_w), BF16),
        scratch_shapes=[
            pltpu.VMEM((rows, heads, dk, dv + LANE), F32),
            pltpu.VMEM((rows, 1, LANE), F32),
        ],
        compiler_params=pltpu.CompilerParams(
            dimension_semantics=("parallel", "arbitrary"), vmem_limit_bytes=VMEM_LIMIT_BYTES),
        name="mlstm_mixer",
    )(q, k, v, gates, tril3)


def _gla_levels(chunk, direct):
    levels, w = [], direct
    while w < chunk:
        levels.append(w)
        w *= 2
    return tuple(levels)


def _gla_segment_matrix(chunk, direct):
    j = np.arange(chunk)[:, None]
    s = np.arange(chunk)[None, :]
    blocks = [s <= j]
    for w in _gla_levels(chunk, direct):
        mid = (j // (2 * w)) * (2 * w) + w - 1
        upper = (j % (2 * w)) >= w
        blocks.append(np.where(upper, (s > mid) & (s <= j), (s > j) & (s <= mid)))
    seg = np.concatenate(blocks, axis=0).astype(np.float32)
    return jnp.asarray(np.concatenate([seg, seg], axis=1), dtype=BF16)


def _gla_kernel(q_ref, k_ref, v_ref, zl_ref, wup_ref, bup_ref, seg_ref, out_ref, s_ref,
                *, heads, dk, dv, chunk, direct):
    L = chunk
    levels = _gla_levels(L, direct)

    @pl.when(pl.program_id(1) == 0)
    def _():
        s_ref[...] = jnp.zeros_like(s_ref)

    row = lax.broadcasted_iota(jnp.int32, (L, L), 0)
    col = lax.broadcasted_iota(jnp.int32, (L, L), 1)
    diff = row ^ col
    level_idx = sum(jnp.where(diff >= w, 1, 0) for w in levels)
    code = jnp.where(col > row, -1, jnp.where(diff < direct, row - col, direct - 1 + level_idx))
    pair_masks = [code == c for c in range(direct + len(levels))]
    row_d = lax.broadcasted_iota(jnp.int32, (L, dk), 0)
    upper_masks = [(row_d & w) != 0 for w in levels]

    def decays(bi):
        z = _dot(zl_ref[bi].astype(BF16), wup_ref[...]) + bup_ref[...]
        la = _log_sigmoid(z) * (LOG2_E / GLA_TAU)
        la_hi = la.astype(BF16)
        la_lo = (la - la_hi.astype(F32)).astype(BF16)
        seg = _dot(seg_ref[...], jnp.concatenate([la_hi, la_lo], axis=0))
        return jnp.exp2(la), seg

    def matrix_stage(bi, h, step, seg):
        lanes = slice(h * dk, (h + 1) * dk)
        b = seg[0:L, lanes]
        g_tot = b[L - 1:L, :]
        qf = q_ref[bi, :, lanes].astype(F32)
        kf = k_ref[bi, :, lanes].astype(F32)
        v = v_ref[bi, :, h * dv:(h + 1) * dv]
        s_prev = s_ref[bi, h]
        o_state = _dot((qf * jnp.exp2(b)).astype(BF16), s_prev.astype(BF16))

        level_products = []
        for i, w in enumerate(levels):
            decay = jnp.exp2(seg[(1 + i) * L:(2 + i) * L, lanes])
            mixed = (jnp.where(upper_masks[i], qf, kf) * decay).astype(BF16)
            level_products.append(_dot_nt(mixed, mixed))

        kg = (kf * jnp.exp2(g_tot - b)).astype(BF16)
        g_col = jnp.broadcast_to(jnp.exp2(g_tot), (LANE, dk)).T
        g_col = jnp.concatenate([g_col] * (dv // LANE), axis=1)
        s_ref[bi, h] = g_col * s_prev + _dot_tn(kg, v)
        return bi, h, qf, kf, step[:, lanes], o_state, level_products

    def vector_stage(bi, h, qf, kf, step_h, o_state, level_products):
        a_mat = jnp.zeros((L, L), F32)
        q3 = qf.reshape(L // SUBLANE, SUBLANE, dk)
        step3 = step_h.reshape(L // SUBLANE, SUBLANE, dk)
        y = kf.reshape(L // SUBLANE, SUBLANE, dk)
        for t in range(direct):
            if t > 0:
                y = step3 * pltpu.roll(y, 1, 1)
            r = jnp.sum(q3 * y, axis=-1, keepdims=True).reshape(L, 1)
            a_mat = jnp.where(pair_masks[t], r, a_mat)
        for i in range(len(levels)):
            a_mat = jnp.where(pair_masks[direct + i], level_products[i], a_mat)
        return bi, h, o_state + _dot(a_mat.astype(BF16), v_ref[bi, :, h * dv:(h + 1) * dv])

    n_rows = q_ref.shape[0]
    in_matrix = in_vector = None
    next_decays = decays(0)
    for bi in range(n_rows):
        step, seg = next_decays
        if bi + 1 < n_rows:
            next_decays = decays(bi + 1)
        for h in range(heads):
            new_matrix = matrix_stage(bi, h, step, seg)
            new_vector = vector_stage(*in_matrix) if in_matrix is not None else None
            if in_vector is not None:
                _gla_epilogue(out_ref,*in_vector, dv)
            in_matrix, in_vector = new_matrix, new_vector
    last_vector = vector_stage(*in_matrix)
    if in_vector is not None:
        _gla_epilogue(out_ref,*in_vector, dv)
    _gla_epilogue(out_ref,*last_vector, dv)


def _gla_epilogue(out_ref, bi, h, o, dv):
    out_ref[bi, :, h * dv:(h + 1) * dv] = o.astype(BF16)


def _gla_mixer(q, k, v, z_low, w_up, b_up, *, heads, chunk, direct, rows):
    b, s, qk_w = q.shape
    v_w = v.shape[2]
    assert s % chunk == 0 and SUBLANE % direct == 0 and b % rows == 0
    dk, dv = qk_w // heads, v_w // heads
    seg_matrix = _gla_segment_matrix(chunk, direct)
    const = lambda i, j: (0, 0)
    block = lambda i, j: (i, j, 0)
    return pl.pallas_call(
        functools.partial(_gla_kernel, heads=heads, dk=dk, dv=dv, chunk=chunk, direct=direct),
        grid=(b // rows, s // chunk),
        in_specs=[
            pl.BlockSpec((rows, chunk, qk_w), block),
            pl.BlockSpec((rows, chunk, qk_w), block),
            pl.BlockSpec((rows, chunk, v_w), block),
            pl.BlockSpec((rows, chunk, LANE), block),
            pl.BlockSpec((LANE, qk_w), const),
            pl.BlockSpec((1, qk_w), const),
            pl.BlockSpec(seg_matrix.shape, const),
        ],
        out_specs=pl.BlockSpec((rows, chunk, v_w), lambda i, j: (i, j, 0)),
        out_shape=jax.ShapeDtypeStruct((b, s, v_w), BF16),
        scratch_shapes=[pltpu.VMEM((rows, heads, dk, dv), F32)],
        compiler_params=pltpu.CompilerParams(
            dimension_semantics=("parallel", "arbitrary"), vmem_limit_bytes=VMEM_LIMIT_BYTES),
        name="gla_mixer",
    )(q, k, v, z_low, w_up, b_up, seg_matrix)


def _gated_head_norm(h_ref, gate_ref, gain_ref, heads, gate_first):
    dv = h_ref.shape[1] // heads
    outs = []
    for h in range(heads):
        cols = slice(h * dv, (h + 1) * dv)
        raw, gate = h_ref[:, cols].astype(F32), gate_ref[:, cols].astype(F32)
        if gate_first:
            y = _rmsnorm(jax.nn.sigmoid(gate) * raw, gain_ref[:, cols])
        else:
            y = gate * jax.nn.sigmoid(gate) * _rmsnorm(raw, gain_ref[:, cols])
        outs.append(y.astype(BF16))
    return jnp.concatenate(outs, axis=1)


def _out_ffn_kernel(h_ref, gate_ref, gain_ref, x_ref, wo_ref, g_ref, w1_ref, w2_ref, fg_ref, o_ref,
                    *, heads, gate_first, ff_chunk, final_norm):
    x1 = x_ref[...] + _dot(_gated_head_norm(h_ref, gate_ref, gain_ref, heads, gate_first), wo_ref[...])
    hn = _rmsnorm(x1, g_ref[...]).astype(BF16)
    acc = x1
    for c in range(w1_ref.shape[1] // ff_chunk):
        u = jnp.maximum(_dot(hn, w1_ref[:, c * ff_chunk:(c + 1) * ff_chunk]), 0.0)
        acc = acc + _dot((u * u).astype(BF16), w2_ref[c * ff_chunk:(c + 1) * ff_chunk, :])
    if final_norm:
        acc = _rmsnorm(acc, fg_ref[...])
    o_ref[...] = acc


def _out_ffn(h2d, gate2d, head_gain, x2d, w_out_all, mixer_layer, norm_g, w1_all, w2_all, layer, final_g,
             *, heads, gate_first, final_norm, tm, ff_chunk):
    t, d = x2d.shape
    v_w, d_ff = h2d.shape[1], w1_all.shape[2]
    const = lambda i: (0, 0)
    resident = pl.Buffered(1)
    return pl.pallas_call(
        functools.partial(_out_ffn_kernel, heads=heads, gate_first=gate_first, ff_chunk=ff_chunk,
                          final_norm=final_norm),
        grid=(t // tm,),
        in_specs=[
            pl.BlockSpec((tm, v_w), lambda i: (i, 0)),
            pl.BlockSpec((tm, v_w), lambda i: (i, 0)),
            pl.BlockSpec((1, v_w), const),
            pl.BlockSpec((tm, d), lambda i: (i, 0)),
            pl.BlockSpec((None, v_w, d), lambda i: (mixer_layer, 0, 0), pipeline_mode=resident),
            pl.BlockSpec((1, d), const),
            pl.BlockSpec((None, d, d_ff), lambda i: (layer, 0, 0), pipeline_mode=resident),
            pl.BlockSpec((None, d_ff, d), lambda i: (layer, 0, 0), pipeline_mode=resident),
            pl.BlockSpec((1, d), const),
        ],
        out_specs=pl.BlockSpec((tm, d), lambda i: (i, 0)),
        out_shape=jax.ShapeDtypeStruct((t, d), F32),
        compiler_params=pltpu.CompilerParams(
            dimension_semantics=("parallel",), vmem_limit_bytes=VMEM_LIMIT_BYTES),
        name="out_proj_ffn",
    )(h2d, gate2d, head_gain, x2d, w_out_all, norm_g, w1_all, w2_all, final_g)


PROJ_ROWS = 1024
FFN_ROWS = 1024
FFN_CHUNK = 1024
MLSTM_CHUNK = 256
GLA_CHUNK = 128
MIXER_ROWS = 8
GLA_DIRECT = 8


def _pad_lanes(a):
    return jnp.pad(a, ((0, 0), (0, LANE - a.shape[1])))


def kernel(x, norm_mix_g, norm_ffn_g, final_norm_g, mlstm_w_in, mlstm_b_gate, mlstm_out_norm_g, mlstm_w_out,
           gla_w_in, gla_w_gate_up, gla_b_gate, gla_out_norm_g, gla_w_out, ffn_w1, ffn_w2):
    b, s, d = x.shape
    depth = norm_mix_g.shape[0]
    heads = N_HEADS
    qk_w, v_w = d // 2, d
    widths = (qk_w, qk_w, v_w, v_w)
    n_main = sum(widths)
    dk = qk_w // heads
    t = b * s
    x2d = x.reshape(t, d)
    final_g = final_norm_g.reshape(1, d)
    n_cols = n_main // qk_w

    mlstm_w_in_b, gla_w_in_b = mlstm_w_in.astype(BF16), gla_w_in.astype(BF16)
    mlstm_w_out_b, gla_w_out_b = mlstm_w_out.astype(BF16), gla_w_out.astype(BF16)
    ffn_w1_b, ffn_w2_b = ffn_w1.astype(BF16), ffn_w2.astype(BF16)

    for i in range(depth):
        j = i // 2
        norm_g = norm_mix_g[i].reshape(1, d)
        if i % 2 == 0:
            scales = (1.0, dk ** -0.5) + (1.0,) * (n_cols - 2)
            q, k, v, out_gate, gates = _input_projection(
                x2d, norm_g, mlstm_w_in_b, j, widths, _pad_lanes(mlstm_w_in[j][:, n_main:]).astype(BF16),
                _pad_lanes(mlstm_b_gate[j].reshape(1, -1)), scales, qk_w, PROJ_ROWS)
            mixed = _mlstm_mixer(*(a.reshape(b, s, -1) for a in (q, k, v)), gates.reshape(b, s, LANE),
                                 heads=heads, chunk=MLSTM_CHUNK, rows=MIXER_ROWS)
            w_out_b, head_gain = mlstm_w_out_b, mlstm_out_norm_g[j].reshape(1, v_w)
        else:
            scales = (dk ** -0.5,) + (1.0,) * (n_cols - 1)
            q, k, v, out_gate, z_low = _input_projection(
                x2d, norm_g, gla_w_in_b, j, widths, _pad_lanes(gla_w_in[j][:, n_main:]).astype(BF16),
                jnp.zeros((1, LANE), F32), scales, qk_w, PROJ_ROWS)
            rank = gla_w_gate_up.shape[1]
            w_up = jnp.pad(gla_w_gate_up[j], ((0, LANE - rank), (0, 0))).astype(BF16)
            mixed = _gla_mixer(*(a.reshape(b, s, -1) for a in (q, k, v)), z_low.reshape(b, s, LANE), w_up,
                               gla_b_gate[j].reshape(1, qk_w),
                               heads=heads, chunk=GLA_CHUNK, direct=GLA_DIRECT, rows=MIXER_ROWS)
            w_out_b, head_gain = gla_w_out_b, gla_out_norm_g[j].reshape(1, v_w)
        x2d = _out_ffn(mixed.reshape(t, v_w), out_gate, head_gain, x2d, w_out_b, j, norm_ffn_g[i].reshape(1, d),
                       ffn_w1_b, ffn_w2_b, i, final_g, heads=heads, gate_first=(i % 2 == 0),
                       final_norm=(i == depth - 1), tm=FFN_ROWS, ff_chunk=FFN_CHUNK)
    return x2d.reshape(b, s, d)
```

```python
import functools

import math

import numpy as np

import jax
import jax.numpy as jnp
from jax import lax
from jax.experimental import pallas as pl
from jax.experimental.pallas import tpu as pltpu

EPS = 1e-6
N_HEADS = 4
GLA_TAU = 16.0
LOG2_E = math.log2(math.e)

LANE = 128
SUBLANE = 8
VMEM_LIMIT_BYTES = 56 * 1024 * 1024

F32 = jnp.float32
BF16 = jnp.bfloat16

_NT = (((1,), (1,)), ((), ()))
_TN = (((0,), (0,)), ((), ()))


def _dot(a, b):
    return jnp.dot(a, b, preferred_element_type=F32)


def _dot_nt(a, b):
    return lax.dot_general(a, b, _NT, preferred_element_type=F32)


def _dot_tn(a, b):
    return lax.dot_general(a, b, _TN, preferred_element_type=F32)


def _rmsnorm(x, g):
    return x * lax.rsqrt(jnp.mean(x * x, axis=-1, keepdims=True) + EPS) * g


def _log_sigmoid(x):
    return jnp.minimum(x, 0.0) - jnp.log(1.0 + jnp.exp(-jnp.abs(x)))


def _cumsum_matrix(chunk):
    tril = np.tril(np.ones((chunk, chunk), np.float32))
    return jnp.asarray(np.concatenate([tril] * 3, axis=1), dtype=BF16)


def _cumsum_rows(tril3, x):
    hi = x.astype(BF16)
    r1 = x - hi.astype(F32)
    mid = r1.astype(BF16)
    lo = (r1 - mid.astype(F32)).astype(BF16)
    return _dot(tril3, jnp.concatenate([hi, mid, lo], axis=0))


def _proj_kernel(x_ref, g_ref, w_ref, wg_ref, bg_ref, *out_refs, col_scales, tn, silu_gate):
    *wide_refs, gate_ref = out_refs
    targets = [(ref, c * tn) for ref in wide_refs for c in range(ref.shape[1] // tn)]
    xb = _rmsnorm(x_ref[...], g_ref[...]).astype(BF16)
    for c, ((ref, start), scale) in enumerate(zip(targets, col_scales)):
        acc = _dot(xb, w_ref[:, c * tn:(c + 1) * tn])
        if scale != 1.0:
            acc = acc * scale
        if ref is wide_refs[-1]:
            acc = acc * jax.nn.sigmoid(acc) if silu_gate else jax.nn.sigmoid(acc)
        ref[:, start:start + tn] = acc.astype(BF16)
    gate_ref[...] = _dot(xb, wg_ref[...]) + bg_ref[...]


def _input_projection(x2d, norm_g, w_in_all, layer, widths, w_gate, b_gate, col_scales, tn, tm, silu_gate):
    t, d = x2d.shape
    n_main = sum(widths)
    assert len(col_scales) * tn == n_main and all(w % tn == 0 for w in widths)
    const = lambda i: (0, 0)
    return pl.pallas_call(
        functools.partial(_proj_kernel, col_scales=col_scales, tn=tn, silu_gate=silu_gate),
        grid=(t // tm,),
        in_specs=[
            pl.BlockSpec((tm, d), lambda i: (i, 0)),
            pl.BlockSpec((1, d), const),
            pl.BlockSpec((None, d, n_main), lambda i: (layer, 0, 0)),
            pl.BlockSpec((d, LANE), const),
            pl.BlockSpec((1, LANE), const),
        ],
        out_specs=[pl.BlockSpec((tm, w), lambda i: (i, 0)) for w in widths]
        + [pl.BlockSpec((tm, LANE), lambda i: (i, 0))],
        out_shape=[jax.ShapeDtypeStruct((t, w), BF16) for w in widths]
        + [jax.ShapeDtypeStruct((t, LANE), F32)],
        compiler_params=pltpu.CompilerParams(
            dimension_semantics=("parallel",), vmem_limit_bytes=VMEM_LIMIT_BYTES),
        name="input_projection",
    )(x2d, norm_g, w_in_all, w_gate, b_gate)


def _cummax_rows(x):
    L = x.shape[0]
    tiles = L // SUBLANE
    x3 = x.reshape(tiles, SUBLANE, LANE)
    sub = lax.broadcasted_iota(jnp.int32, x3.shape, 1)
    shift = 1
    while shift < SUBLANE:
        x3 = jnp.maximum(x3, jnp.where(sub >= shift, pltpu.roll(x3, shift, 1), -jnp.inf))
        shift *= 2
    tot = jnp.broadcast_to(x3[:, SUBLANE - 1:SUBLANE, :], x3.shape)
    before = jnp.concatenate([jnp.full((1, SUBLANE, LANE), -jnp.inf, F32), tot[:tiles - 1]], axis=0)
    shift = 1
    while shift < tiles:
        pad = jnp.full((shift, SUBLANE, LANE), -jnp.inf, F32)
        before = jnp.maximum(before, jnp.concatenate([pad, before[:tiles - shift]], axis=0))
        shift *= 2
    return jnp.maximum(x3, before).reshape(L, LANE)


def _mlstm_kernel(q_ref, k_ref, v_ref, gt_ref, tril_ref, out_ref, c_ref, m_ref,
                  *, heads, dk, dv, chunk):
    L = chunk

    @pl.when(pl.program_id(1) == 0)
    def _():
        c_ref[...] = jnp.zeros_like(c_ref)
        m_ref[...] = jnp.zeros_like(m_ref)

    causal = lax.broadcasted_iota(jnp.int32, (L, L), 1) <= lax.broadcasted_iota(jnp.int32, (L, L), 0)
    lane = lax.broadcasted_iota(jnp.int32, (L, LANE), 1)
    head_lanes = lane < heads
    ones_col = [jnp.where(lane == h, 1.0, 0.0).astype(BF16) for h in range(heads)]

    def gate_terms(bi):
        gates = gt_ref[bi]
        pre = jnp.where(head_lanes, gates, _log_sigmoid(gates)) * LOG2_E
        cum = _cumsum_rows(tril_ref[...], pre)
        rows_t = jnp.where(head_lanes, pre, cum).T
        li = jnp.where(head_lanes, pre, 0.0)
        b = jnp.where(head_lanes, pltpu.roll(cum, LANE - heads, 1), 0.0)
        m_prev = m_ref[bi]
        g_tot = b[L - 1:L, :]

        mm = jnp.maximum(m_prev, _cummax_rows(li - b))
        w_inter = jnp.exp2(m_prev - mm)

        a = g_tot - b + li
        m_new = jnp.maximum(g_tot + m_prev, jnp.max(a, axis=0, keepdims=True))
        w_s = jnp.exp2(a - m_new)
        decay = jnp.exp2(g_tot + m_prev - m_new)
        m_ref[bi] = m_new
        return rows_t, b, mm, w_inter, w_s, decay

    n_rows = q_ref.shape[0]
    pending = None
    next_terms = gate_terms(0)
    for bi in range(n_rows):
        rows_t, b, mm, w_inter, w_s, decay = next_terms
        if bi + 1 < n_rows:
            next_terms = gate_terms(bi + 1)

        nums, den = [], jnp.zeros((L, LANE), F32)
        for h in range(heads):
            q = q_ref[bi, :, h * dk:(h + 1) * dk]
            k = k_ref[bi, :, h * dk:(h + 1) * dk]
            v_ext = jnp.concatenate([v_ref[bi, :, h * dv:(h + 1) * dv], ones_col[h]], axis=1)
            c_row = rows_t[h:h + 1, :] - rows_t[heads + h:heads + h + 1, :]
            p = jnp.where(causal, jnp.exp2(c_row - mm[:, h:h + 1]), 0.0)
            s = (_dot_nt(q, k) * p).astype(BF16)
            q_w = (q.astype(F32) * w_inter[:, h:h + 1]).astype(BF16)
            state = c_ref[bi, h]
            num = _dot(q_w, state.astype(BF16)) + _dot(s, v_ext)
            nums.append(num[:, :dv])
            den = den + num[:, dv:]

            kw = (k.astype(F32) * w_s[:, h:h + 1]).astype(BF16)
            c_ref[bi, h] = decay[:, h:h + 1] * state + _dot_tn(kw, v_ext)

            if pending is not None:
                _mlstm_epilogue(out_ref, *pending, h, dv)

        inv = 1.0 / jnp.maximum(jnp.abs(den), jnp.exp2(-(b + mm)))
        pending = (bi, nums, inv)
    for h in range(heads):
        _mlstm_epilogue(out_ref, *pending, h, dv)


def _mlstm_epilogue(out_ref, bi, nums, inv, h, dv):
    out_ref[bi, :, h * dv:(h + 1) * dv] = (nums[h] * inv[:, h:h + 1]).astype(BF16)


def _mlstm_mixer(q, k, v, gates, *, heads, chunk, rows):
    b, s, qk_w = q.shape
    v_w = v.shape[2]
    assert s % chunk == 0 and b % rows == 0 and 2 * heads <= LANE
    dk, dv = qk_w // heads, v_w // heads
    tril3 = _cumsum_matrix(chunk)
    const = lambda i, j: (0, 0)
    block = lambda i, j: (i, j, 0)
    return pl.pallas_call(
        functools.partial(_mlstm_kernel, heads=heads, dk=dk, dv=dv, chunk=chunk),
        grid=(b // rows, s // chunk),
        in_specs=[
            pl.BlockSpec((rows, chunk, qk_w), block),
            pl.BlockSpec((rows, chunk, qk_w), block),
            pl.BlockSpec((rows, chunk, v_w), block),
            pl.BlockSpec((rows, chunk, LANE), block),
            pl.BlockSpec(tril3.shape, const),
        ],
        out_specs=pl.BlockSpec((rows, chunk, v_w), lambda i, j: (i, j, 0)),
        out_shape=jax.ShapeDtypeStruct((b, s, v_w), BF16),
        scratch_shapes=[
            pltpu.VMEM((rows, heads, dk, dv + LANE), F32),
            pltpu.VMEM((rows, 1, LANE), F32),
        ],
        compiler_params=pltpu.CompilerParams(
            dimension_semantics=("parallel", "arbitrary"), vmem_limit_bytes=VMEM_LIMIT_BYTES),
        name="mlstm_mixer",
    )(q, k, v, gates, tril3)


def _gla_levels(chunk, direct):
    levels, w = [], direct
    while w < chunk:
        levels.append(w)
        w *= 2
    return tuple(levels)


def _gla_segment_matrix(chunk, direct):
    j = np.arange(chunk)[:, None]
    s = np.arange(chunk)[None, :]
    blocks = [s <= j]
    for w in _gla_levels(chunk, direct):
        mid = (j // (2 * w)) * (2 * w) + w - 1
        upper = (j % (2 * w)) >= w
        blocks.append(np.where(upper, (s > mid) & (s <= j), (s > j) & (s <= mid)))
    seg = np.concatenate(blocks, axis=0).astype(np.float32)
    return jnp.asarray(np.concatenate([seg, seg], axis=1), dtype=BF16)


def _gla_kernel(q_ref, k_ref, v_ref, zl_ref, wup_ref, bup_ref, seg_ref, out_ref, s_ref,
                *, heads, dk, dv, chunk, direct):
    L = chunk
    levels = _gla_levels(L, direct)

    @pl.when(pl.program_id(1) == 0)
    def _():
        s_ref[...] = jnp.zeros_like(s_ref)

    row = lax.broadcasted_iota(jnp.int32, (L, L), 0)
    col = lax.broadcasted_iota(jnp.int32, (L, L), 1)
    diff = row ^ col
    level_idx = sum(jnp.where(diff >= w, 1, 0) for w in levels)
    code = jnp.where(col > row, -1, jnp.where(diff < direct, row - col, direct - 1 + level_idx))
    pair_masks = [code == c for c in range(direct + len(levels))]
    row_d = lax.broadcasted_iota(jnp.int32, (L, dk), 0)
    upper_masks = [(row_d & w) != 0 for w in levels]

    def decays(bi):
        z = _dot(zl_ref[bi].astype(BF16), wup_ref[...]) + bup_ref[...]
        la = _log_sigmoid(z) * (LOG2_E / GLA_TAU)
        la_hi = la.astype(BF16)
        la_lo = (la - la_hi.astype(F32)).astype(BF16)
        seg = _dot(seg_ref[...], jnp.concatenate([la_hi, la_lo], axis=0))
        return jnp.exp2(la), seg

    def matrix_stage(bi, h, step, seg):
        lanes = slice(h * dk, (h + 1) * dk)
        b = seg[0:L, lanes]
        g_tot = b[L - 1:L, :]
        qf = q_ref[bi, :, lanes].astype(F32)
        kf = k_ref[bi, :, lanes].astype(F32)
        v = v_ref[bi, :, h * dv:(h + 1) * dv]
        s_prev = s_ref[bi, h]
        o_state = _dot((qf * jnp.exp2(b)).astype(BF16), s_prev.astype(BF16))

        level_products = []
        for i, w in enumerate(levels):
            decay = jnp.exp2(seg[(1 + i) * L:(2 + i) * L, lanes])
            mixed = (jnp.where(upper_masks[i], qf, kf) * decay).astype(BF16)
            level_products.append(_dot_nt(mixed, mixed))

        kg = (kf * jnp.exp2(g_tot - b)).astype(BF16)
        g_col = jnp.broadcast_to(jnp.exp2(g_tot), (LANE, dk)).T
        g_col = jnp.concatenate([g_col] * (dv // LANE), axis=1)
        s_ref[bi, h] = g_col * s_prev + _dot_tn(kg, v)
        return bi, h, qf, kf, step[:, lanes], o_state, level_products

    def vector_stage(bi, h, qf, kf, step_h, o_state, level_products):
        a_mat = jnp.zeros((L, L), F32)
        q3 = qf.reshape(L // SUBLANE, SUBLANE, dk)
        step3 = step_h.reshape(L // SUBLANE, SUBLANE, dk)
        y = kf.reshape(L // SUBLANE, SUBLANE, dk)
        for t in range(direct):
            if t > 0:
                y = step3 * pltpu.roll(y, 1, 1)
            r = jnp.sum(q3 * y, axis=-1, keepdims=True).reshape(L, 1)
            a_mat = jnp.where(pair_masks[t], r, a_mat)
        for i in range(len(levels)):
            a_mat = jnp.where(pair_masks[direct + i], level_products[i], a_mat)
        return bi, h, o_state + _dot(a_mat.astype(BF16), v_ref[bi, :, h * dv:(h + 1) * dv])

    n_rows = q_ref.shape[0]
    in_matrix = in_vector = None
    next_decays = decays(0)
    for bi in range(n_rows):
        step, seg = next_decays
        if bi + 1 < n_rows:
            next_decays = decays(bi + 1)
        for h in range(heads):
            new_matrix = matrix_stage(bi, h, step, seg)
            new_vector = vector_stage(*in_matrix) if in_matrix is not None else None
            if in_vector is not None:
                _gla_epilogue(out_ref,*in_vector, dv)
            in_matrix, in_vector = new_matrix, new_vector
    last_vector = vector_stage(*in_matrix)
    if in_vector is not None:
        _gla_epilogue(out_ref,*in_vector, dv)
    _gla_epilogue(out_ref,*last_vector, dv)


def _gla_epilogue(out_ref, bi, h, o, dv):
    out_ref[bi, :, h * dv:(h + 1) * dv] = o.astype(BF16)


def _gla_mixer(q, k, v, z_low, w_up, b_up, *, heads, chunk, direct, rows):
    b, s, qk_w = q.shape
    v_w = v.shape[2]
    assert s % chunk == 0 and SUBLANE % direct == 0 and b % rows == 0
    dk, dv = qk_w // heads, v_w // heads
    seg_matrix = _gla_segment_matrix(chunk, direct)
    const = lambda i, j: (0, 0)
    block = lambda i, j: (i, j, 0)
    return pl.pallas_call(
        functools.partial(_gla_kernel, heads=heads, dk=dk, dv=dv, chunk=chunk, direct=direct),
        grid=(b // rows, s // chunk),
        in_specs=[
            pl.BlockSpec((rows, chunk, qk_w), block),
            pl.BlockSpec((rows, chunk, qk_w), block),
            pl.BlockSpec((rows, chunk, v_w), block),
            pl.BlockSpec((rows, chunk, LANE), block),
            pl.BlockSpec((LANE, qk_w), const),
            pl.BlockSpec((1, qk_w), const),
            pl.BlockSpec(seg_matrix.shape, const),
        ],
        out_specs=pl.BlockSpec((rows, chunk, v_w), lambda i, j: (i, j, 0)),
        out_shape=jax.ShapeDtypeStruct((b, s, v_w), BF16),
        scratch_shapes=[pltpu.VMEM((rows, heads, dk, dv), F32)],
        compiler_params=pltpu.CompilerParams(
            dimension_semantics=("parallel", "arbitrary"), vmem_limit_bytes=VMEM_LIMIT_BYTES),
        name="gla_mixer",
    )(q, k, v, z_low, w_up, b_up, seg_matrix)


def _gated_head_norm(h_ref, gate_ref, gain_ref, heads, gate_first):
    dv = h_ref.shape[1] // heads
    outs = []
    for h in range(heads):
        cols = slice(h * dv, (h + 1) * dv)
        raw, gate = h_ref[:, cols].astype(F32), gate_ref[:, cols].astype(F32)
        if gate_first:
            y = _rmsnorm(gate * raw, gain_ref[:, cols])
        else:
            y = gate * _rmsnorm(raw, gain_ref[:, cols])
        outs.append(y.astype(BF16))
    return jnp.concatenate(outs, axis=1)


def _out_ffn_kernel(h_ref, gate_ref, gain_ref, x_ref, wo_ref, g_ref, w1_ref, w2_ref, fg_ref, o_ref,
                    *, heads, gate_first, ff_chunk, final_norm):
    x1 = x_ref[...] + _dot(_gated_head_norm(h_ref, gate_ref, gain_ref, heads, gate_first), wo_ref[...])
    hn = _rmsnorm(x1, g_ref[...]).astype(BF16)
    acc = x1
    for c in range(w1_ref.shape[1] // ff_chunk):
        u = jnp.maximum(_dot(hn, w1_ref[:, c * ff_chunk:(c + 1) * ff_chunk]), 0.0)
        acc = acc + _dot((u * u).astype(BF16), w2_ref[c * ff_chunk:(c + 1) * ff_chunk, :])
    if final_norm:
        acc = _rmsnorm(acc, fg_ref[...])
    o_ref[...] = acc


def _out_ffn(h2d, gate2d, head_gain, x2d, w_out_all, mixer_layer, norm_g, w1_all, w2_all, layer, final_g,
             *, heads, gate_first, final_norm, tm, ff_chunk):
    t, d = x2d.shape
    v_w, d_ff = h2d.shape[1], w1_all.shape[2]
    const = lambda i: (0, 0)
    resident = pl.Buffered(1)
    return pl.pallas_call(
        functools.partial(_out_ffn_kernel, heads=heads, gate_first=gate_first, ff_chunk=ff_chunk,
                          final_norm=final_norm),
        grid=(t // tm,),
        in_specs=[
            pl.BlockSpec((tm, v_w), lambda i: (i, 0)),
            pl.BlockSpec((tm, v_w), lambda i: (i, 0)),
            pl.BlockSpec((1, v_w), const),
            pl.BlockSpec((tm, d), lambda i: (i, 0)),
            pl.BlockSpec((None, v_w, d), lambda i: (mixer_layer, 0, 0), pipeline_mode=resident),
            pl.BlockSpec((1, d), const),
            pl.BlockSpec((None, d, d_ff), lambda i: (layer, 0, 0), pipeline_mode=resident),
            pl.BlockSpec((None, d_ff, d), lambda i: (layer, 0, 0), pipeline_mode=resident),
            pl.BlockSpec((1, d), const),
        ],
        out_specs=pl.BlockSpec((tm, d), lambda i: (i, 0)),
        out_shape=jax.ShapeDtypeStruct((t, d), F32),
        compiler_params=pltpu.CompilerParams(
            dimension_semantics=("parallel",), vmem_limit_bytes=VMEM_LIMIT_BYTES),
        name="out_proj_ffn",
    )(h2d, gate2d, head_gain, x2d, w_out_all, norm_g, w1_all, w2_all, final_g)


PROJ_ROWS = 1024
FFN_ROWS = 1024
FFN_CHUNK = 1024
MLSTM_CHUNK = 256
GLA_CHUNK = 128
MIXER_ROWS = 8
GLA_DIRECT = 8


def _pad_lanes(a):
    return jnp.pad(a, ((0, 0), (0, LANE - a.shape[1])))


def kernel(x, norm_mix_g, norm_ffn_g, final_norm_g, mlstm_w_in, mlstm_b_gate, mlstm_out_norm_g, mlstm_w_out,
           gla_w_in, gla_w_gate_up, gla_b_gate, gla_out_norm_g, gla_w_out, ffn_w1, ffn_w2):
    b, s, d = x.shape
    depth = norm_mix_g.shape[0]
    heads = N_HEADS
    qk_w, v_w = d // 2, d
    widths = (qk_w, qk_w, v_w, v_w)
    n_main = sum(widths)
    dk = qk_w // heads
    t = b * s
    x2d = x.reshape(t, d)
    final_g = final_norm_g.reshape(1, d)
    n_cols = n_main // qk_w

    mlstm_w_in_b, gla_w_in_b = mlstm_w_in.astype(BF16), gla_w_in.astype(BF16)
    mlstm_w_out_b, gla_w_out_b = mlstm_w_out.astype(BF16), gla_w_out.astype(BF16)
    ffn_w1_b, ffn_w2_b = ffn_w1.astype(BF16), ffn_w2.astype(BF16)

    for i in range(depth):
        j = i // 2
        norm_g = norm_mix_g[i].reshape(1, d)
        if i % 2 == 0:
            scales = (1.0, dk ** -0.5) + (1.0,) * (n_cols - 2)
            q, k, v, out_gate, gates = _input_projection(
                x2d, norm_g, mlstm_w_in_b, j, widths, _pad_lanes(mlstm_w_in[j][:, n_main:]).astype(BF16),
                _pad_lanes(mlstm_b_gate[j].reshape(1, -1)), scales, qk_w, PROJ_ROWS, False)
            mixed = _mlstm_mixer(*(a.reshape(b, s, -1) for a in (q, k, v)), gates.reshape(b, s, LANE),
                                 heads=heads, chunk=MLSTM_CHUNK, rows=MIXER_ROWS)
            w_out_b, head_gain = mlstm_w_out_b, mlstm_out_norm_g[j].reshape(1, v_w)
        else:
            scales = (dk ** -0.5,) + (1.0,) * (n_cols - 1)
            q, k, v, out_gate, z_low = _input_projection(
                x2d, norm_g, gla_w_in_b, j, widths, _pad_lanes(gla_w_in[j][:, n_main:]).astype(BF16),
                jnp.zeros((1, LANE), F32), scales, qk_w, PROJ_ROWS, True)
            rank = gla_w_gate_up.shape[1]
            w_up = jnp.pad(gla_w_gate_up[j], ((0, LANE - rank), (0, 0))).astype(BF16)
            mixed = _gla_mixer(*(a.reshape(b, s, -1) for a in (q, k, v)), z_low.reshape(b, s, LANE), w_up,
                               gla_b_gate[j].reshape(1, qk_w),
                               heads=heads, chunk=GLA_CHUNK, direct=GLA_DIRECT, rows=MIXER_ROWS)
            w_out_b, head_gain = gla_w_out_b, gla_out_norm_g[j].reshape(1, v_w)
        x2d = _out_ffn(mixed.reshape(t, v_w), out_gate, head_gain, x2d, w_out_b, j, norm_ffn_g[i].reshape(1, d),
                       ffn_w1_b, ffn_w2_b, i, final_g, heads=heads, gate_first=(i % 2 == 0),
                       final_norm=(i == depth - 1), tm=FFN_ROWS, ff_chunk=FFN_CHUNK)
    return x2d.reshape(b, s, d)
```
